```python
import math
import jax, jax.numpy as jnp
from jax import lax
import numpy as np

D_MODEL = 1024
BATCH = 8
SEQ = 4096
DEPTH = 2

HEAD_DIM = 64
ROPE_THETA = 10000.0
SWA_HEADS = 16
SWA_KV_HEADS = 4
SWA_WINDOW = 128
SWA_BLOCK = 128
SWA_Q = SWA_HEADS * HEAD_DIM
SWA_KV = SWA_KV_HEADS * HEAD_DIM
SWA_IN = SWA_Q + 2 * SWA_KV
NSA_HEADS = 16
NSA_KV_HEADS = 4
CMP_LEN = 32
CMP_STRIDE = 16
CMP_HIDDEN = 256
SEL_BLOCK = 64
SEL_TOPN = 16
NSA_WINDOW = 512
NSA_QCHUNK = 32
SEL_FORCE = 1e4
NSA_Q = NSA_HEADS * HEAD_DIM
NSA_KV = NSA_KV_HEADS * HEAD_DIM
NSA_IN = NSA_Q + 6 * NSA_KV + 3 * NSA_HEADS
D_FF = 3584
N_EXPERTS = 8
TOP_K = 2
DN_ALPHA = (2 * DEPTH) ** 0.25
DN_BETA = (8 * DEPTH) ** -0.25
LN_EPS = 1e-5
NEG_INF = -1e30
N_EVEN = (DEPTH + 1) // 2
N_ODD = DEPTH // 2

kernel_name = 'hybrid_swa_sink_nsa_moe_deepnorm'


def _layer_norm(x, g, b):
    xf = x.astype(jnp.float32)
    mu = jnp.mean(xf, axis=-1, keepdims=True)
    var = jnp.mean(jnp.square(xf - mu), axis=-1, keepdims=True)
    return ((xf - mu) * lax.rsqrt(var + LN_EPS)).astype(x.dtype) * g + b


def _rope_tables(seq):
    inv = 1.0 / (ROPE_THETA ** (jnp.arange(0, HEAD_DIM, 2, dtype=jnp.float32) / HEAD_DIM))
    ang = jnp.arange(seq, dtype=jnp.float32)[:, None] * inv[None, :]
    return jnp.cos(ang), jnp.sin(ang)


def _rope(x, cos, sin):
    c = cos[None, :, None, :].astype(x.dtype)
    s = sin[None, :, None, :].astype(x.dtype)
    x1, x2 = jnp.split(x, 2, axis=-1)
    return jnp.concatenate([x1 * c - x2 * s, x2 * c + x1 * s], axis=-1)


def _masked_softmax(s, mask):
    s = jnp.where(mask, s, NEG_INF)
    m = jnp.max(s, axis=-1, keepdims=True)
    p = jnp.where(mask, jnp.exp(s - m), 0.0)
    return p / jnp.maximum(jnp.sum(p, axis=-1, keepdims=True), 1e-30)


def _swa_sink_attention(x, w_in, w_out, sinks, cos, sin):
    B, S, _ = x.shape
    G = SWA_HEADS // SWA_KV_HEADS
    nb = S // SWA_BLOCK
    proj = x @ w_in
    q, k, v = jnp.split(proj, [SWA_Q, SWA_Q + SWA_KV], axis=-1)
    q = _rope(q.reshape(B, S, SWA_HEADS, HEAD_DIM), cos, sin)
    k = _rope(k.reshape(B, S, SWA_KV_HEADS, HEAD_DIM), cos, sin)
    v = v.reshape(B, S, SWA_KV_HEADS, HEAD_DIM)
    qb = q.reshape(B, nb, SWA_BLOCK, SWA_KV_HEADS, G, HEAD_DIM)
    kb = k.reshape(B, nb, SWA_BLOCK, SWA_KV_HEADS, HEAD_DIM)
    vb = v.reshape(B, nb, SWA_BLOCK, SWA_KV_HEADS, HEAD_DIM)
    kk = jnp.concatenate([jnp.concatenate([jnp.zeros_like(kb[:, :1]), kb[:, :-1]], axis=1), kb], axis=2)
    vv = jnp.concatenate([jnp.concatenate([jnp.zeros_like(vb[:, :1]), vb[:, :-1]], axis=1), vb], axis=2)
    scores = jnp.einsum('bnqkgd,bnskd->bnkgqs', qb, kk).astype(jnp.float32) * (HEAD_DIM ** -0.5)
    n_i = jnp.arange(nb)[:, None, None]
    q_i = jnp.arange(SWA_BLOCK)[None, :, None] + SWA_BLOCK
    s_i = jnp.arange(2 * SWA_BLOCK)[None, None, :]
    diff = q_i - s_i
    abs_s = n_i * SWA_BLOCK - SWA_BLOCK + s_i
    mask = ((diff >= 0) & (diff < SWA_WINDOW) & (abs_s >= 0))[None, :, None, None]
    sink = sinks.astype(jnp.float32).reshape(1, 1, SWA_KV_HEADS, G, 1, 1)
    s = jnp.where(mask, scores, NEG_INF)
    m = jnp.maximum(jnp.max(s, axis=-1, keepdims=True), sink)
    p = jnp.where(mask, jnp.exp(s - m), 0.0)
    probs = p / (jnp.sum(p, axis=-1, keepdims=True) + jnp.exp(sink - m))
    o = jnp.einsum('bnkgqs,bnskd->bnqkgd', probs.astype(vv.dtype), vv)
    return o.reshape(B, S, SWA_Q) @ w_out


def _overlap_matrix(nc, nsel):
    cs = CMP_STRIDE * np.arange(nc)
    ce = cs + CMP_LEN
    ss = SEL_BLOCK * np.arange(nsel)
    se = ss + SEL_BLOCK
    ov = np.clip(np.minimum(ce[:, None], se[None, :]) - np.maximum(cs[:, None], ss[None, :]), 0, None)
    return jnp.asarray(ov / CMP_STRIDE, dtype=jnp.float32)


def _nsa_attention(x, w_in, w_out, pos_k, pos_v, ck_w1, ck_w2, cv_w1, cv_w2, cos, sin):
    B, S, _ = x.shape
    Hkv = NSA_KV_HEADS
    G = NSA_HEADS // NSA_KV_HEADS
    scale = HEAD_DIM ** -0.5
    proj = x @ w_in
    splits = [NSA_Q + i * NSA_KV for i in range(7)]
    q, kc, vc, ks, vs, kw, vw, gl = jnp.split(proj, splits, axis=-1)
    q = _rope(q.reshape(B, S, NSA_HEADS, HEAD_DIM), cos, sin)
    kc = _rope(kc.reshape(B, S, Hkv, HEAD_DIM), cos, sin)
    ks = _rope(ks.reshape(B, S, Hkv, HEAD_DIM), cos, sin)
    kw = _rope(kw.reshape(B, S, Hkv, HEAD_DIM), cos, sin)
    vc = vc.reshape(B, S, Hkv, HEAD_DIM)
    vs = vs.reshape(B, S, Hkv, HEAD_DIM)
    vw = vw.reshape(B, S, Hkv, HEAD_DIM)
    gates = jax.nn.sigmoid(gl).reshape(B, S, Hkv, G, 3)

    nc = (S - CMP_LEN) // CMP_STRIDE + 1
    idx = CMP_STRIDE * np.arange(nc)[:, None] + np.arange(CMP_LEN)[None, :]

    def compress(t, pos, w1, w2):
        blk = t[:, idx] + pos[None, None, :, None, :]
        flat = blk.transpose(0, 1, 3, 2, 4).reshape(B, nc, Hkv, CMP_LEN * HEAD_DIM)
        return jax.nn.gelu(flat @ w1) @ w2

    kcc = compress(kc, pos_k, ck_w1, ck_w2)
    vcc = compress(vc, pos_v, cv_w1, cv_w2)
    cmp_end = jnp.asarray(idx[:, -1])

    nsel = S // SEL_BLOCK
    topn = min(SEL_TOPN, nsel)
    ov = _overlap_matrix(nc, nsel)
    kblk = ks.reshape(B, nsel, SEL_BLOCK, Hkv, HEAD_DIM).transpose(0, 3, 1, 2, 4)
    vblk = vs.reshape(B, nsel, SEL_BLOCK, Hkv, HEAD_DIM).transpose(0, 3, 1, 2, 4)
    b_i = jnp.arange(B)[:, None, None, None]
    h_i = jnp.arange(Hkv)[None, :, None, None]
    blk_ids = jnp.arange(nsel)

    kw_pad = jnp.pad(kw, ((0, 0), (NSA_WINDOW, 0), (0, 0), (0, 0)))
    vw_pad = jnp.pad(vw, ((0, 0), (NSA_WINDOW, 0), (0, 0), (0, 0)))

    def chunk(ci):
        c0 = ci * NSA_QCHUNK
        tpos = c0 + jnp.arange(NSA_QCHUNK)
        qc = lax.dynamic_slice_in_dim(q, c0, NSA_QCHUNK, axis=1).reshape(B, NSA_QCHUNK, Hkv, G, HEAD_DIM)
        s_c = jnp.einsum('bqkgd,bckd->bkgqc', qc, kcc).astype(jnp.float32) * scale
        p_c = _masked_softmax(s_c, cmp_end[None, :] <= tpos[:, None])
        o_c = jnp.einsum('bkgqc,bckd->bqkgd', p_c.astype(vcc.dtype), vcc)
        imp = jnp.sum(p_c, axis=2) @ ov
        cur = (tpos // SEL_BLOCK)[:, None]
        forced = (blk_ids[None, :] == 0) | (blk_ids[None, :] == cur) | (blk_ids[None, :] == cur - 1)
        imp = jnp.where(forced, SEL_FORCE, imp)
        imp = jnp.where(blk_ids[None, :] <= cur, imp, -1.0)
        _, sel = lax.top_k(imp, topn)
        ksel = kblk[b_i, h_i, sel].reshape(B, Hkv, NSA_QCHUNK, topn * SEL_BLOCK, HEAD_DIM)
        vsel = vblk[b_i, h_i, sel].reshape(B, Hkv, NSA_QCHUNK, topn * SEL_BLOCK, HEAD_DIM)
        spos = (sel[..., None] * SEL_BLOCK + jnp.arange(SEL_BLOCK)).reshape(B, Hkv, NSA_QCHUNK, topn * SEL_BLOCK)
        mask_s = (spos <= tpos[None, None, :, None])[:, :, None]
        s_s = jnp.einsum('bqkgd,bkqsd->bkgqs', qc, ksel).astype(jnp.float32) * scale
        p_s = _masked_softmax(s_s, mask_s)
        o_s = jnp.einsum('bkgqs,bkqsd->bqkgd', p_s.astype(vsel.dtype), vsel)
        kwc = lax.dynamic_slice_in_dim(kw_pad, c0, NSA_WINDOW + NSA_QCHUNK, axis=1)
        vwc = lax.dynamic_slice_in_dim(vw_pad, c0, NSA_WINDOW + NSA_QCHUNK, axis=1)
        wpos = c0 - NSA_WINDOW + jnp.arange(NSA_WINDOW + NSA_QCHUNK)
        diff = tpos[:, None] - wpos[None, :]
        mask_w = (diff >= 0) & (diff < NSA_WINDOW) & (wpos[None, :] >= 0)
        s_w = jnp.einsum('bqkgd,bskd->bkgqs', qc, kwc).astype(jnp.float32) * scale
        p_w = _masked_softmax(s_w, mask_w)
        o_w = jnp.einsum('bkgqs,bskd->bqkgd', p_w.astype(vwc.dtype), vwc)
        g = lax.dynamic_slice_in_dim(gates, c0, NSA_QCHUNK, axis=1)
        o = g[..., 0:1] * o_c + g[..., 1:2] * o_s + g[..., 2:3] * o_w
        return o.reshape(B, NSA_QCHUNK, NSA_Q)

    outs = lax.map(chunk, jnp.arange(S // NSA_QCHUNK))
    o = outs.transpose(1, 0, 2, 3).reshape(B, S, NSA_Q)
    return o @ w_out


def _swiglu(x, w_gate, w_up, w_down):
    return (jax.nn.silu(x @ w_gate) * (x @ w_up)) @ w_down


def _moe(x, router, w_gate, w_up, w_down):
    B, S, D = x.shape
    xt = x.reshape(B * S, D)
    logits = (xt @ router).astype(jnp.float32)
    vals, ids = lax.top_k(logits, TOP_K)
    w = jax.nn.softmax(vals, axis=-1)
    gate = jnp.sum(jax.nn.one_hot(ids, N_EXPERTS, dtype=jnp.float32) * w[..., None], axis=1)
    out = jnp.zeros_like(xt)
    for e in range(N_EXPERTS):
        out = out + gate[:, e:e + 1].astype(xt.dtype) * _swiglu(xt, w_gate[e], w_up[e], w_down[e])
    return out.reshape(B, S, D)


def _normal(k, shape, scale):
    return jax.random.normal(k, shape, jnp.float32) * scale


def setup_inputs(seed: int = 0) -> dict:
    key = jax.random.key(seed)
    ks = jax.random.split(key, 24)
    D = D_MODEL
    a_col = jnp.concatenate([jnp.ones((SWA_Q + SWA_KV,), jnp.float32), jnp.full((SWA_KV,), DN_BETA, jnp.float32)])
    b_col = jnp.concatenate([
        jnp.ones((NSA_Q + NSA_KV,), jnp.float32),
        jnp.full((NSA_KV,), DN_BETA, jnp.float32),
        jnp.ones((NSA_KV,), jnp.float32),
        jnp.full((NSA_KV,), DN_BETA, jnp.float32),
        jnp.ones((NSA_KV,), jnp.float32),
        jnp.full((NSA_KV,), DN_BETA, jnp.float32),
        jnp.ones((3 * NSA_HEADS,), jnp.float32)])
    return {
        'x': _normal(ks[0], (BATCH, SEQ, D), 1.0),
        'a_w_in': _normal(ks[1], (N_EVEN, D, SWA_IN), D ** -0.5) * a_col,
        'a_w_out': _normal(ks[2], (N_EVEN, SWA_Q, D), DN_BETA * SWA_Q ** -0.5),
        'a_sinks': _normal(ks[3], (N_EVEN, SWA_HEADS), 1.0),
        'b_w_in': _normal(ks[4], (N_ODD, D, NSA_IN), D ** -0.5) * b_col,
        'b_w_out': _normal(ks[5], (N_ODD, NSA_Q, D), DN_BETA * NSA_Q ** -0.5),
        'b_cmp_pos_k': _normal(ks[6], (N_ODD, CMP_LEN, HEAD_DIM), 0.1),
        'b_cmp_pos_v': _normal(ks[7], (N_ODD, CMP_LEN, HEAD_DIM), 0.1),
        'b_cmp_k_w1': _normal(ks[8], (N_ODD, CMP_LEN * HEAD_DIM, CMP_HIDDEN), (CMP_LEN * HEAD_DIM) ** -0.5),
        'b_cmp_k_w2': _normal(ks[9], (N_ODD, CMP_HIDDEN, HEAD_DIM), CMP_HIDDEN ** -0.5),
        'b_cmp_v_w1': _normal(ks[10], (N_ODD, CMP_LEN * HEAD_DIM, CMP_HIDDEN), (CMP_LEN * HEAD_DIM) ** -0.5),
        'b_cmp_v_w2': _normal(ks[11], (N_ODD, CMP_HIDDEN, HEAD_DIM), CMP_HIDDEN ** -0.5),
        'ffn_w_gate': _normal(ks[12], (N_EVEN, D, D_FF), D ** -0.5),
        'ffn_w_up': _normal(ks[13], (N_EVEN, D, D_FF), D ** -0.5),
        'ffn_w_down': _normal(ks[14], (N_EVEN, D_FF, D), DN_BETA * D_FF ** -0.5),
        'moe_router': _normal(ks[15], (N_ODD, D, N_EXPERTS), D ** -0.5),
        'moe_w_gate': _normal(ks[16], (N_ODD, N_EXPERTS, D, D_FF), D ** -0.5),
        'moe_w_up': _normal(ks[17], (N_ODD, N_EXPERTS, D, D_FF), D ** -0.5),
        'moe_w_down': _normal(ks[18], (N_ODD, N_EXPERTS, D_FF, D), DN_BETA * D_FF ** -0.5),
        'ln_gain': 1.0 + _normal(ks[19], (DEPTH, 2, D), 0.02),
        'ln_bias': _normal(ks[20], (DEPTH, 2, D), 0.02),
    }


def reference(x, a_w_in, a_w_out, a_sinks, b_w_in, b_w_out, b_cmp_pos_k, b_cmp_pos_v,
              b_cmp_k_w1, b_cmp_k_w2, b_cmp_v_w1, b_cmp_v_w2, ffn_w_gate, ffn_w_up, ffn_w_down,
              moe_router, moe_w_gate, moe_w_up, moe_w_down, ln_gain, ln_bias):
    cos, sin = _rope_tables(x.shape[1])
    for i in range(DEPTH):
        j = i // 2
        if i % 2 == 0:
            h = _swa_sink_attention(x, a_w_in[j], a_w_out[j], a_sinks[j], cos, sin)
        else:
            h = _nsa_attention(x, b_w_in[j], b_w_out[j], b_cmp_pos_k[j], b_cmp_pos_v[j],
                               b_cmp_k_w1[j], b_cmp_k_w2[j], b_cmp_v_w1[j], b_cmp_v_w2[j], cos, sin)
        x = _layer_norm(DN_ALPHA * x + h, ln_gain[i, 0], ln_bias[i, 0])
        if i % 2 == 0:
            f = _swiglu(x, ffn_w_gate[j], ffn_w_up[j], ffn_w_down[j])
        else:
            f = _moe(x, moe_router[j], moe_w_gate[j], moe_w_up[j], moe_w_down[j])
        x = _layer_norm(DN_ALPHA * x + f, ln_gain[i, 1], ln_bias[i, 1])
    return x
```

```python
import functools
import math

import numpy as np
import jax
import jax.numpy as jnp
from jax import lax
from jax.experimental import pallas as pl
from jax.experimental.pallas import tpu as pltpu

F32 = jnp.float32
BF16 = jnp.bfloat16

HEAD_DIM = 64
HALF_DIM = HEAD_DIM // 2
LANES = 128
ROPE_THETA = 10000.0
KV_HEADS = 4
GROUP = 4
SWA_BLOCK = 128
SWA_WINDOW = 128
CMP_LEN = 32
CMP_STRIDE = 16
SEL_BLOCK = 64
SEL_SHIFT = 6
SEL_TOPN = 16
NSA_WINDOW = 512
SEL_FORCE = 1e4
N_GATES = 3
LN_EPS = 1e-5
NEG_INF = -1e30
VMEM_LIMIT = 48 * 1024 * 1024

NSA_TQ = 128
NSA_KC = 512


def _cparams(*sem):
    return pltpu.CompilerParams(dimension_semantics=sem, vmem_limit_bytes=VMEM_LIMIT)


def _dot(a, b):
    return jnp.dot(a, b, preferred_element_type=F32)


def _dot_nt(a, b):
    return lax.dot_general(a, b, (((1,), (1,)), ((), ())), preferred_element_type=F32)


def _dot_tn(a, b):
    return lax.dot_general(a, b, (((0,), (0,)), ((), ())), preferred_element_type=F32)


def _pair_cols(n_heads):
    idx = []
    for j in range(n_heads // 2):
        a, b = 2 * j * HEAD_DIM, (2 * j + 1) * HEAD_DIM
        idx += list(range(a, a + HALF_DIM)) + list(range(b, b + HALF_DIM))
        idx += list(range(a + HALF_DIM, a + HEAD_DIM)) + list(range(b + HALF_DIM, b + HEAD_DIM))
    return np.asarray(idx, np.int32)


def _kdup_cols(n_heads):
    idx = []
    for h in range(n_heads):
        a = h * HEAD_DIM
        idx += list(range(a, a + HALF_DIM)) * 2 + list(range(a + HALF_DIM, a + HEAD_DIM)) * 2
    return np.asarray(idx, np.int32)


def _vdup_cols(n_heads):
    idx = []
    for h in range(n_heads):
        idx += list(range(h * HEAD_DIM, (h + 1) * HEAD_DIM)) * 2
    return np.asarray(idx, np.int32)


def _rope_tables(seq):
    inv = 1.0 / (ROPE_THETA ** (jnp.arange(0, HEAD_DIM, 2, dtype=F32) / HEAD_DIM))
    ang = jnp.arange(seq, dtype=F32)[:, None] * inv[None, :]
    cos, sin = jnp.cos(ang), jnp.sin(ang)
    return jnp.tile(cos, (1, 4)), jnp.concatenate([-sin, -sin, sin, sin], axis=1)


def _rope(y, c, s):
    return y * c + pltpu.roll(y, LANES // 2, 1) * s


def _layer_norm(z, g, b):
    mu = jnp.mean(z, axis=-1, keepdims=True)
    zc = z - mu
    var = jnp.mean(zc * zc, axis=-1, keepdims=True)
    return zc * lax.rsqrt(var + LN_EPS) * g + b


def _proj_a_kernel(x_ref, w_ref, cos_ref, sin_ref, q_ref, k_ref, v_ref):
    x = x_ref[...].astype(BF16)
    c, s = cos_ref[...], sin_ref[...]
    nq = q_ref.shape[1]
    nk = k_ref.shape[1]
    for j in range(nq // LANES):
        y = _dot(x, w_ref[:, j * LANES:(j + 1) * LANES])
        q_ref[:, j * LANES:(j + 1) * LANES] = _rope(y, c, s).astype(BF16)
    for j in range(nk // LANES):
        y = _dot(x, w_ref[:, nq + j * LANES:nq + (j + 1) * LANES])
        k_ref[:, j * LANES:(j + 1) * LANES] = _rope(y, c, s).astype(BF16)
    v_ref[...] = _dot(x, w_ref[:, nq + nk:]).astype(BF16)


def _proj_a(x2d, w, cos, sin, seq, tm=512):
    t, d = x2d.shape
    nq = d
    nk = KV_HEADS * LANES
    per_seq = seq // tm
    return pl.pallas_call(
        _proj_a_kernel,
        grid=(t // tm,),
        in_specs=[
            pl.BlockSpec((tm, d), lambda i: (i, 0)),
            pl.BlockSpec(w.shape, lambda i: (0, 0)),
            pl.BlockSpec((tm, LANES), lambda i: (i % per_seq, 0)),
            pl.BlockSpec((tm, LANES), lambda i: (i % per_seq, 0)),
        ],
        out_specs=[
            pl.BlockSpec((tm, nq), lambda i: (i, 0)),
            pl.BlockSpec((tm, nk), lambda i: (i, 0)),
            pl.BlockSpec((tm, nk), lambda i: (i, 0)),
        ],
        out_shape=[
            jax.ShapeDtypeStruct((t, nq), BF16),
            jax.ShapeDtypeStruct((t, nk), BF16),
            jax.ShapeDtypeStruct((t, nk), BF16),
        ],
        compiler_params=_cparams("parallel"),
        name="proj_a",
    )(x2d, w, cos, sin)


def _proj_b_kernel(x_ref, w_ref, cos_ref, sin_ref,
                   q_ref, kc_ref, ks_ref, kw_ref, vc_ref, vs_ref, vw_ref, g_ref):
    x = x_ref[...].astype(BF16)
    c, s = cos_ref[...], sin_ref[...]
    col = 0
    for ref in (q_ref, kc_ref, ks_ref, kw_ref):
        for j in range(ref.shape[1] // LANES):
            y = _dot(x, w_ref[:, col:col + LANES])
            ref[:, j * LANES:(j + 1) * LANES] = _rope(y, c, s).astype(ref.dtype)
            col += LANES
    for ref in (vc_ref, vs_ref, vw_ref):
        n = ref.shape[1]
        ref[...] = _dot(x, w_ref[:, col:col + n]).astype(ref.dtype)
        col += n
    g_ref[...] = jax.nn.sigmoid(_dot(x, w_ref[:, col:]))


def _proj_b(x2d, w, cos, sin, seq, tm=512):
    t, d = x2d.shape
    nkv = KV_HEADS * HEAD_DIM
    ndup = KV_HEADS * LANES
    widths = [(d, BF16), (nkv, F32), (ndup, BF16), (ndup, BF16),
              (nkv, F32), (ndup, BF16), (ndup, BF16), (KV_HEADS * LANES, F32)]
    per_seq = seq // tm
    return pl.pallas_call(
        _proj_b_kernel,
        grid=(t // tm,),
        in_specs=[
            pl.BlockSpec((tm, d), lambda i: (i, 0)),
            pl.BlockSpec(w.shape, lambda i: (0, 0)),
            pl.BlockSpec((tm, LANES), lambda i: (i % per_seq, 0)),
            pl.BlockSpec((tm, LANES), lambda i: (i % per_seq, 0)),
        ],
        out_specs=[pl.BlockSpec((tm, n), lambda i: (i, 0)) for n, _ in widths],
        out_shape=[jax.ShapeDtypeStruct((t, n), dt) for n, dt in widths],
        compiler_params=_cparams("parallel"),
        name="proj_b",
    )(x2d, w, cos, sin)


def _stack_heads(q_ref, rows):
    lane = lax.broadcasted_iota(jnp.int32, (rows, LANES), 1)
    first = (lane & (HEAD_DIM - 1)) < HALF_DIM
    zero = jnp.zeros((rows, LANES), BF16)
    parts = []
    for p in range(GROUP // 2):
        qp = q_ref[:, p * LANES:(p + 1) * LANES]
        parts += [jnp.where(first, qp, zero), jnp.where(first, zero, qp)]
    return jnp.concatenate(parts, axis=0)


def _unstack_heads(o, rows):
    lane = lax.broadcasted_iota(jnp.int32, (rows, LANES), 1)
    low = lane < HEAD_DIM
    return [jnp.where(low, o[2 * p], o[2 * p + 1]) for p in range(GROUP // 2)]


def _swa_kernel(sink_ref, q_ref, kp_ref, kc_ref, vp_ref, vc_ref, o_ref):
    n = pl.program_id(1)
    blk = SWA_BLOCK
    qi = lax.broadcasted_iota(jnp.int32, (blk, 2 * blk), 0)
    si = lax.broadcasted_iota(jnp.int32, (blk, 2 * blk), 1)
    diff = qi + blk - si
    ok = (diff >= 0) & (diff < SWA_WINDOW) & ((si >= blk) | (n > 0))
    for g in range(KV_HEADS):
        q4 = _stack_heads(q_ref.at[:, g * 2 * LANES:(g + 1) * 2 * LANES], blk)
        kk = jnp.concatenate([kp_ref[:, g * LANES:(g + 1) * LANES], kc_ref[:, g * LANES:(g + 1) * LANES]], axis=0)
        vv = jnp.concatenate([vp_ref[:, g * LANES:(g + 1) * LANES], vc_ref[:, g * LANES:(g + 1) * LANES]], axis=0)
        s = _dot_nt(q4, kk).reshape(GROUP, blk, 2 * blk)
        s = jnp.where(ok[None], s, NEG_INF)
        ps, rs = [], []
        for h in range(GROUP):
            sink = sink_ref[g * GROUP + h]
            m = jnp.maximum(jnp.max(s[h], axis=-1, keepdims=True), sink)
            p = jnp.exp(s[h] - m)
            rs.append(1.0 / (jnp.sum(p, axis=-1, keepdims=True) + jnp.exp(sink - m)))
            ps.append(p.astype(BF16))
        o = _dot(jnp.concatenate(ps, axis=0), vv).reshape(GROUP, blk, LANES)
        o = [o[h] * rs[h] for h in range(GROUP)]
        for p, blkout in enumerate(_unstack_heads(o, blk)):
            c0 = (g * 2 + p) * LANES
            o_ref[:, c0:c0 + LANES] = blkout.astype(BF16)


def _swa_attention(q, k2, v2, sinks, batch, seq):
    t, nq = q.shape
    nk = k2.shape[1]
    nb = seq // SWA_BLOCK
    cur = lambda b, n: (b * nb + n, 0)
    prev = lambda b, n: (b * nb + jnp.maximum(n - 1, 0), 0)
    return pl.pallas_call(
        _swa_kernel,
        grid=(batch, nb),
        in_specs=[
            pl.BlockSpec(memory_space=pltpu.SMEM),
            pl.BlockSpec((SWA_BLOCK, nq), cur),
            pl.BlockSpec((SWA_BLOCK, nk), prev),
            pl.BlockSpec((SWA_BLOCK, nk), cur),
            pl.BlockSpec((SWA_BLOCK, nk), prev),
            pl.BlockSpec((SWA_BLOCK, nk), cur),
        ],
        out_specs=pl.BlockSpec((SWA_BLOCK, nq), cur),
        out_shape=jax.ShapeDtypeStruct((t, nq), BF16),
        compiler_params=_cparams("parallel", "parallel"),
        name="swa_attention",
    )(sinks, q, k2, k2, v2, v2)


def _outproj_ln_kernel(alpha, o_ref, w_ref, x_ref, g_ref, b_ref, y_ref, ybf_ref):
    z = alpha * x_ref[...].astype(F32) + _dot(o_ref[...], w_ref[...])
    y = _layer_norm(z, g_ref[...], b_ref[...])
    y_ref[...] = y
    ybf_ref[...] = y.astype(BF16)


def _outproj_ln_router_kernel(alpha, o_ref, w_ref, x_ref, g_ref, b_ref, r_ref, y_ref, ybf_ref, route_ref):
    z = alpha * x_ref[...].astype(F32) + _dot(o_ref[...], w_ref[...])
    y = _layer_norm(z, g_ref[...], b_ref[...])
    y_ref[...] = y
    ybf_ref[...] = y.astype(BF16)
    logits = lax.dot_general(r_ref[...], y, (((1,), (1,)), ((), ())),
                             precision=lax.Precision.HIGHEST, preferred_element_type=F32)
    ne = logits.shape[0]
    eid = lax.broadcasted_iota(jnp.int32, logits.shape, 0)
    m1 = jnp.max(logits, axis=0, keepdims=True)
    i1 = jnp.min(jnp.where(logits == m1, eid, ne), axis=0, keepdims=True)
    rest = jnp.where(eid == i1, -jnp.inf, logits)
    m2 = jnp.max(rest, axis=0, keepdims=True)
    i2 = jnp.min(jnp.where(rest == m2, eid, ne), axis=0, keepdims=True)
    e = jnp.exp(m2 - m1)
    w1 = 1.0 / (1.0 + e)
    w2 = e / (1.0 + e)
    route_ref[...] = jnp.concatenate(
        [i1.astype(F32), i2.astype(F32), w1, w2, jnp.zeros((4, logits.shape[1]), F32)], axis=0)


def _outproj_ln(o, w, x, gain, bias, alpha, router_t=None, tm=512):
    t, d = x.shape
    row = lambda i: (i, 0)
    const = lambda i: (0, 0)
    in_specs = [
        pl.BlockSpec((tm, o.shape[1]), row),
        pl.BlockSpec(w.shape, const),
        pl.BlockSpec((tm, d), row),
        pl.BlockSpec((1, d), const),
        pl.BlockSpec((1, d), const),
    ]
    out_specs = [pl.BlockSpec((tm, d), row), pl.BlockSpec((tm, d), row)]
    out_shape = [jax.ShapeDtypeStruct((t, d), F32), jax.ShapeDtypeStruct((t, d), BF16)]
    args = [o, w, x, gain.reshape(1, d), bias.reshape(1, d)]
    if router_t is None:
        body = functools.partial(_outproj_ln_kernel, alpha)
        name = "outproj_ln"
    else:
        body = functools.partial(_outproj_ln_router_kernel, alpha)
        name = "outproj_ln_router"
        in_specs.append(pl.BlockSpec(router_t.shape, const))
        args.append(router_t)
        out_specs.append(pl.BlockSpec((8, tm), lambda i: (0, i)))
        out_shape.append(jax.ShapeDtypeStruct((8, t), F32))
    return pl.pallas_call(
        body,
        grid=(t // tm,),
        in_specs=in_specs,
        out_specs=out_specs,
        out_shape=out_shape,
        compiler_params=_cparams("parallel"),
        name=name,
    )(*args)


def _ffn_ln_kernel(alpha, xbf_ref, x_ref, wg_ref, wu_ref, wd_ref, g_ref, b_ref, y_ref, ybf_ref, acc_ref):
    j = pl.program_id(1)

    @pl.when(j == 0)
    def _():
        acc_ref[...] = jnp.zeros_like(acc_ref)

    xb = xbf_ref[...]
    h = jax.nn.silu(_dot(xb, wg_ref[...])) * _dot(xb, wu_ref[...])
    acc_ref[...] += _dot(h.astype(BF16), wd_ref[...])

    @pl.when(j == pl.num_programs(1) - 1)
    def _():
        y = _layer_norm(alpha * x_ref[...] + acc_ref[...], g_ref[...], b_ref[...])
        y_ref[...] = y
        ybf_ref[...] = y.astype(BF16)


def _ffn_ln(xbf, x, wg, wu, wd, gain, bias, alpha, tm=512, tf=512):
    t, d = x.shape
    dff = wg.shape[1]
    row = lambda i, j: (i, 0)
    const = lambda i, j: (0, 0)
    return pl.pallas_call(
        functools.partial(_ffn_ln_kernel, alpha),
        grid=(t // tm, dff // tf),
        in_specs=[
            pl.BlockSpec((tm, d), row),
            pl.BlockSpec((tm, d), row),
            pl.BlockSpec((d, tf), lambda i, j: (0, j)),
            pl.BlockSpec((d, tf), lambda i, j: (0, j)),
            pl.BlockSpec((tf, d), lambda i, j: (j, 0)),
            pl.BlockSpec((1, d), const),
            pl.BlockSpec((1, d), const),
        ],
        out_specs=[pl.BlockSpec((tm, d), row), pl.BlockSpec((tm, d), row)],
        out_shape=[jax.ShapeDtypeStruct((t, d), F32), jax.ShapeDtypeStruct((t, d), BF16)],
        scratch_shapes=[pltpu.VMEM((tm, d), F32)],
        compiler_params=_cparams("parallel", "arbitrary"),
        name="ffn_ln",
    )(xbf, x, wg, wu, wd, gain.reshape(1, d), bias.reshape(1, d))


def _gelu_tanh(x):
    return 0.5 * x * (1.0 + jnp.tanh(math.sqrt(2.0 / math.pi) * (x + 0.044715 * (x * x * x))))


def _compress_kernel(kc_ref, vc_ref, pk_ref, pv_ref, w1k_ref, w1v_ref, w2k_ref, w2v_ref, ko_ref, vo_ref):
    nseg = ko_ref.shape[0]
    half = CMP_LEN // 2

    def one(t_ref, pos_ref, w1_ref, w2_ref, out_ref):
        hidden = w1_ref.shape[2]
        lo = jnp.zeros((nseg, hidden), F32)
        hi = jnp.zeros((nseg, hidden), F32)
        for l in range(half):
            rows = t_ref[pl.ds(l, nseg, stride=CMP_STRIDE), :]
            lo += _dot((rows + pos_ref[l:l + 1, :]).astype(BF16), w1_ref[l])
            hi += _dot((rows + pos_ref[half + l:half + l + 1, :]).astype(BF16), w1_ref[half + l])
        h = _gelu_tanh(lo + pltpu.roll(hi, nseg - 1, 0))
        out_ref[...] = _dot(h.astype(BF16), w2_ref[...]).astype(out_ref.dtype)

    one(kc_ref, pk_ref, w1k_ref, w2k_ref, ko_ref)
    one(vc_ref, pv_ref, w1v_ref, w2v_ref, vo_ref)


def _compress(kc, vc, pk, pv, w1k, w1v, w2k, w2v, batch, seq):
    nseg = seq // CMP_STRIDE
    npair = KV_HEADS // 2
    tok = lambda b, p: (b, p)
    c2 = lambda b, p: (0, 0)
    c3 = lambda b, p: (0, 0, 0)
    return pl.pallas_call(
        _compress_kernel,
        grid=(batch, npair),
        in_specs=[
            pl.BlockSpec((seq, LANES), tok),
            pl.BlockSpec((seq, LANES), tok),
            pl.BlockSpec(pk.shape, c2),
            pl.BlockSpec(pv.shape, c2),
            pl.BlockSpec(w1k.shape, c3),
            pl.BlockSpec(w1v.shape, c3),
            pl.BlockSpec(w2k.shape, c2),
            pl.BlockSpec(w2v.shape, c2),
        ],
        out_specs=[pl.BlockSpec((nseg, 2 * LANES), tok), pl.BlockSpec((nseg, 2 * LANES), tok)],
        out_shape=[jax.ShapeDtypeStruct((batch * nseg, KV_HEADS * LANES), BF16)] * 2,
        compiler_params=_cparams("parallel", "parallel"),
        name="nsa_compress",
    )(kc, vc, pk, pv, w1k, w1v, w2k, w2v)


def _nsa_kernel(q_ref, gate_ref, kcc_ref, vcc_ref, ks_ref, vs_ref, kw_ref, vw_ref, ovt_ref, exp_ref,
                o_ref, imp_ref, m_ref, l_ref, acc_ref):
    tq, kc = NSA_TQ, NSA_KC
    seq = ks_ref.shape[0]
    ncp = kcc_ref.shape[0]
    nsel = ovt_ref.shape[0]
    topn = min(SEL_TOPN, nsel)
    t0 = pl.program_id(2) * tq
    q4 = _stack_heads(q_ref, tq)
    tpos = t0 + lax.broadcasted_iota(jnp.int32, (tq, 1), 0)

    cend = CMP_STRIDE * lax.broadcasted_iota(jnp.int32, (1, ncp), 1) + (CMP_LEN - 1)
    ok_c = (cend <= tpos)[None]
    s = jnp.where(ok_c, _dot_nt(q4, kcc_ref[...]).reshape(GROUP, tq, ncp), NEG_INF)
    m = jnp.max(s, axis=-1, keepdims=True)
    p = jnp.where(ok_c, jnp.exp(s - m), 0.0)
    p = p / jnp.maximum(jnp.sum(p, axis=-1, keepdims=True), 1e-30)
    o_c = _dot(p.reshape(GROUP * tq, ncp).astype(BF16), vcc_ref[...]).reshape(GROUP, tq, LANES)

    psum = p[0] + p[1] + p[2] + p[3]
    ovt = ovt_ref[...]
    imp = jnp.zeros((nsel, tq), F32)
    rem = psum
    for _ in range(3):
        part = rem.astype(BF16)
        imp += _dot_nt(ovt, part)
        rem = rem - part.astype(F32)
    blk = lax.broadcasted_iota(jnp.int32, (nsel, tq), 0)
    cur = jnp.right_shift(t0 + lax.broadcasted_iota(jnp.int32, (nsel, tq), 1), SEL_SHIFT)
    forced = (blk == 0) | (blk == cur) | (blk == cur - 1)
    imp = jnp.where(forced, SEL_FORCE, imp)
    imp = jnp.where(blk <= cur, imp, -1.0)
    imp_ref[...] = imp

    ngrp = nsel // 8
    sub = lax.broadcasted_iota(jnp.int32, (8, tq), 0)
    grp = [imp[8 * r:8 * r + 8] for r in range(ngrp)]
    cnt = [jnp.zeros((8, tq), F32) for _ in range(ngrp)]
    for i in range(nsel):
        row = jnp.broadcast_to(imp_ref[i:i + 1, :], (8, tq))
        for r in range(ngrp):
            ge = jnp.where(row >= grp[r], 1.0, 0.0)
            gt = jnp.where(row > grp[r], 1.0, 0.0)
            if 8 * r > i:
                cnt[r] = cnt[r] + ge
            elif 8 * r + 7 < i:
                cnt[r] = cnt[r] + gt
            else:
                cnt[r] = cnt[r] + jnp.where(sub > i - 8 * r, ge, gt)
    rank = jnp.concatenate(cnt, axis=0)
    sel_t = jnp.where((rank < topn) & (blk <= cur), 1.0, 0.0).astype(BF16)

    m_ref[...] = jnp.full(m_ref.shape, NEG_INF, F32)
    l_ref[...] = jnp.zeros(l_ref.shape, F32)
    acc_ref[...] = jnp.zeros(acc_ref.shape, F32)
    for c in range(seq // kc):

        @pl.when(c * kc < t0 + tq)
        def _():
            picked = _dot_tn(sel_t, exp_ref[c])
            kpos = c * kc + lax.broadcasted_iota(jnp.int32, (1, kc), 1)
            ok = ((picked > 0.5) & (kpos <= tpos))[None]
            sc = _dot_nt(q4, ks_ref[c * kc:(c + 1) * kc, :]).reshape(GROUP, tq, kc)
            sc = jnp.where(ok, sc, NEG_INF)
            m_old = m_ref[...]
            m_new = jnp.maximum(m_old, jnp.max(sc, axis=-1, keepdims=True))
            a = jnp.exp(m_old - m_new)
            pc = jnp.exp(sc - m_new)
            l_ref[...] = a * l_ref[...] + jnp.sum(pc, axis=-1, keepdims=True)
            pv = _dot(pc.reshape(GROUP * tq, kc).astype(BF16), vs_ref[c * kc:(c + 1) * kc, :])
            acc_ref[...] = a * acc_ref[...] + pv.reshape(GROUP, tq, LANES)
            m_ref[...] = m_new

    span = NSA_WINDOW + tq
    w0 = pl.multiple_of(jnp.maximum(t0 - NSA_WINDOW, 0), tq)
    kpos = w0 + lax.broadcasted_iota(jnp.int32, (1, span), 1)
    d = tpos - kpos
    ok_w = ((d >= 0) & (d < NSA_WINDOW))[None]
    sw = _dot_nt(q4, kw_ref[pl.ds(w0, span), :]).reshape(GROUP, tq, span)
    sw = jnp.where(ok_w, sw, NEG_INF)
    pw = jnp.exp(sw - jnp.max(sw, axis=-1, keepdims=True))
    lw = jnp.sum(pw, axis=-1, keepdims=True)
    o_w = _dot(pw.reshape(GROUP * tq, span).astype(BF16), vw_ref[pl.ds(w0, span), :]).reshape(GROUP, tq, LANES)

    gates = gate_ref[...]
    outs = []
    for h in range(GROUP):
        gc = gates[:, N_GATES * h:N_GATES * h + 1]
        gs = gates[:, N_GATES * h + 1:N_GATES * h + 2]
        gw = gates[:, N_GATES * h + 2:N_GATES * h + 3]
        outs.append(gc * o_c[h] + (gs / l_ref[h]) * acc_ref[h] + (gw / lw[h]) * o_w[h])
    for pidx, blkout in enumerate(_unstack_heads(outs, tq)):
        o_ref[:, pidx * LANES:(pidx + 1) * LANES] = blkout.astype(BF16)


def _nsa_attention(q, gates, kcc, vcc, ks2, vs2, kw2, vw2, ovt, expand, batch, seq):
    t, nq = q.shape
    tq = NSA_TQ
    nt = seq // tq
    ncp = seq // CMP_STRIDE
    nsel = seq // SEL_BLOCK
    qmap = lambda b, g, i: (b * nt + i, g)
    kvmap = lambda b, g, i: (b, g)
    return pl.pallas_call(
        _nsa_kernel,
        grid=(batch, KV_HEADS, nt),
        in_specs=[
            pl.BlockSpec((tq, 2 * LANES), qmap),
            pl.BlockSpec((tq, LANES), qmap),
            pl.BlockSpec((ncp, LANES), kvmap),
            pl.BlockSpec((ncp, LANES), kvmap),
            pl.BlockSpec((seq, LANES), kvmap),
            pl.BlockSpec((seq, LANES), kvmap),
            pl.BlockSpec((seq, LANES), kvmap),
            pl.BlockSpec((seq, LANES), kvmap),
            pl.BlockSpec(ovt.shape, lambda b, g, i: (0, 0)),
            pl.BlockSpec(expand.shape, lambda b, g, i: (0, 0, 0)),
        ],
        out_specs=pl.BlockSpec((tq, 2 * LANES), qmap),
        out_shape=jax.ShapeDtypeStruct((t, nq), BF16),
        scratch_shapes=[
            pltpu.VMEM((nsel, tq), F32),
            pltpu.VMEM((GROUP, tq, 1), F32),
            pltpu.VMEM((GROUP, tq, 1), F32),
            pltpu.VMEM((GROUP, tq, LANES), F32),
        ],
        compiler_params=_cparams("parallel", "parallel", "arbitrary"),
        name="nsa_attention",
    )(q, gates, kcc, vcc, ks2, vs2, kw2, vw2, ovt, expand)


def _moe_kernel(te_ref, nu_ref, xs_ref, wg_ref, wu_ref, wd_ref, ys_ref, acc_ref):
    i = pl.program_id(0)
    j = pl.program_id(1)

    @pl.when(i < nu_ref[0])
    def _():
        @pl.when(j == 0)
        def _():
            acc_ref[...] = jnp.zeros_like(acc_ref)

        xb = xs_ref[...]
        h = jax.nn.silu(_dot(xb, wg_ref[0])) * _dot(xb, wu_ref[0])
        acc_ref[...] += _dot(h.astype(BF16), wd_ref[0])

        @pl.when(j == pl.num_programs(1) - 1)
        def _():
            ys_ref[...] = acc_ref[...].astype(ys_ref.dtype)


def _moe_ffn(xs, wg, wu, wd, tile_expert, n_used, tm, tf=512):
    p, d = xs.shape
    dff = wg.shape[2]
    nj = dff // tf

    def rows(i, j, te, nu):
        return (jnp.minimum(i, nu[0] - 1), 0)

    def jj(i, j, nu):
        return jnp.where(i < nu[0], j, nj - 1)

    return pl.pallas_call(
        _moe_kernel,
        grid_spec=pltpu.PrefetchScalarGridSpec(
            num_scalar_prefetch=2,
            grid=(p // tm, nj),
            in_specs=[
                pl.BlockSpec((tm, d), rows),
                pl.BlockSpec((1, d, tf), lambda i, j, te, nu: (te[i], 0, jj(i, j, nu))),
                pl.BlockSpec((1, d, tf), lambda i, j, te, nu: (te[i], 0, jj(i, j, nu))),
                pl.BlockSpec((1, tf, d), lambda i, j, te, nu: (te[i], jj(i, j, nu), 0)),
            ],
            out_specs=pl.BlockSpec((tm, d), rows),
            scratch_shapes=[pltpu.VMEM((tm, d), F32)],
        ),
        out_shape=jax.ShapeDtypeStruct((p, d), BF16),
        compiler_params=_cparams("arbitrary", "arbitrary"),
        name="moe_ffn",
    )(tile_expert, n_used, xs, wg, wu, wd)


def _combine_ln_kernel(alpha, x_ref, y1_ref, y2_ref, w_ref, g_ref, b_ref, o_ref, obf_ref):
    w = w_ref[...]
    f = w[:, 0:1] * y1_ref[...].astype(F32) + w[:, 1:2] * y2_ref[...].astype(F32)
    y = _layer_norm(alpha * x_ref[...] + f, g_ref[...], b_ref[...])
    o_ref[...] = y
    obf_ref[...] = y.astype(BF16)


def _combine_ln(x, y1, y2, wcol, gain, bias, alpha, tm=512):
    t, d = x.shape
    row = lambda i: (i, 0)
    const = lambda i: (0, 0)
    return pl.pallas_call(
        functools.partial(_combine_ln_kernel, alpha),
        grid=(t // tm,),
        in_specs=[
            pl.BlockSpec((tm, d), row),
            pl.BlockSpec((tm, d), row),
            pl.BlockSpec((tm, d), row),
            pl.BlockSpec((tm, 2), row),
            pl.BlockSpec((1, d), const),
            pl.BlockSpec((1, d), const),
        ],
        out_specs=[pl.BlockSpec((tm, d), row), pl.BlockSpec((tm, d), row)],
        out_shape=[jax.ShapeDtypeStruct((t, d), F32), jax.ShapeDtypeStruct((t, d), BF16)],
        compiler_params=_cparams("parallel"),
        name="combine_ln",
    )(x, y1, y2, wcol, gain.reshape(1, d), bias.reshape(1, d))


def _moe(x, xbf, route, wg, wu, wd, gain, bias, alpha, tm=512):
    t, d = x.shape
    ne = wg.shape[0]
    ids = route[0:2].astype(jnp.int32).reshape(-1)
    tok = jnp.tile(jnp.arange(t, dtype=jnp.int32), 2)
    onehot = (ids[:, None] == jnp.arange(ne, dtype=jnp.int32)[None, :]).astype(jnp.int32)
    csum = jnp.cumsum(onehot, axis=0)
    rank = jnp.sum(onehot * (csum - 1), axis=1)
    counts = csum[-1]
    padded = ((counts + tm - 1) // tm) * tm
    ends = jnp.cumsum(padded)
    starts = ends - padded
    pos = jnp.sum(onehot * starts[None, :], axis=1) + rank
    n_rows = 2 * t + ne * tm
    n_tiles = n_rows // tm
    src = jnp.zeros((n_rows,), jnp.int32).at[pos].set(tok, unique_indices=True)
    tile_start = jnp.arange(n_tiles, dtype=jnp.int32) * tm
    tile_expert = jnp.minimum(jnp.sum((tile_start[:, None] >= ends[None, :]).astype(jnp.int32), axis=1), ne - 1)
    n_used = (ends[-1] // tm).astype(jnp.int32)
    last_expert = tile_expert[jnp.maximum(n_used - 1, 0)]
    tile_expert = jnp.where(jnp.arange(n_tiles) < n_used, tile_expert, last_expert).astype(jnp.int32)
    xs = jnp.take(xbf, src, axis=0)
    ys = _moe_ffn(xs, wg, wu, wd, tile_expert, n_used.reshape(1), tm)
    y1 = jnp.take(ys, pos[:t], axis=0)
    y2 = jnp.take(ys, pos[t:], axis=0)
    wcol = route[2:4].T
    return _combine_ln(x, y1, y2, wcol, gain, bias, alpha)


def _prep_a_w_in(w):
    d = w.shape[0]
    nq = d
    nkv = KV_HEADS * HEAD_DIM
    scale = HEAD_DIM ** -0.5
    q = w[:, :nq][:, _pair_cols(nq // HEAD_DIM)] * scale
    k = w[:, nq:nq + nkv][:, _kdup_cols(KV_HEADS)]
    v = w[:, nq + nkv:][:, _vdup_cols(KV_HEADS)]
    return jnp.concatenate([q, k, v], axis=1).astype(BF16)


def _prep_b_w_in(w):
    d = w.shape[0]
    nq = d
    nkv = KV_HEADS * HEAD_DIM
    scale = HEAD_DIM ** -0.5
    part = lambda i: w[:, nq + i * nkv:nq + (i + 1) * nkv]
    q = w[:, :nq][:, _pair_cols(nq // HEAD_DIM)] * scale
    kc = part(0)[:, _pair_cols(KV_HEADS)]
    vc = part(1)
    ks = part(2)[:, _kdup_cols(KV_HEADS)]
    vs = part(3)[:, _vdup_cols(KV_HEADS)]
    kw = part(4)[:, _kdup_cols(KV_HEADS)]
    vw = part(5)[:, _vdup_cols(KV_HEADS)]
    gl = w[:, nq + 6 * nkv:]
    per = GROUP * N_GATES
    gcols = []
    for g in range(KV_HEADS):
        gcols.append(jnp.pad(gl[:, g * per:(g + 1) * per], ((0, 0), (0, LANES - per))))
    return jnp.concatenate([q, kc, ks, kw, vc, vs, vw] + gcols, axis=1).astype(BF16)


def _prep_compress(pos, w1, w2, rope_layout):
    hidden = w1.shape[1]
    w1 = w1.reshape(CMP_LEN, HEAD_DIM, hidden)
    zeros = jnp.zeros_like(w1)
    a = jnp.concatenate([w1, zeros], axis=2)
    b = jnp.concatenate([zeros, w1], axis=2)
    natural = jnp.concatenate([a, b], axis=1)
    pos2 = jnp.concatenate([pos, pos], axis=1)
    if rope_layout:
        cols = _pair_cols(2)
        natural, pos2 = natural[:, cols, :], pos2[:, cols]
        out_cols = _kdup_cols(1)
    else:
        out_cols = _vdup_cols(1)
    w2d = w2[:, out_cols]
    z2 = jnp.zeros_like(w2d)
    w2p = jnp.concatenate([jnp.concatenate([w2d, z2], axis=1), jnp.concatenate([z2, w2d], axis=1)], axis=0)
    return pos2.astype(F32), natural.astype(BF16), w2p.astype(BF16)


def _overlap_t(ncp, nsel):
    cs = CMP_STRIDE * np.arange(ncp)
    ce = cs + CMP_LEN
    ss = SEL_BLOCK * np.arange(nsel)
    se = ss + SEL_BLOCK
    ov = np.clip(np.minimum(ce[None, :], se[:, None]) - np.maximum(cs[None, :], ss[:, None]), 0, None)
    return jnp.asarray(ov / CMP_STRIDE, dtype=BF16)


def _expand(nsel, seq):
    key_blk = (np.arange(seq) // SEL_BLOCK).reshape(seq // NSA_KC, 1, NSA_KC)
    return jnp.asarray(key_blk == np.arange(nsel)[None, :, None], dtype=BF16)


def kernel(x, a_w_in, a_w_out, a_sinks, b_w_in, b_w_out, b_cmp_pos_k, b_cmp_pos_v, b_cmp_k_w1, b_cmp_k_w2, b_cmp_v_w1, b_cmp_v_w2, ffn_w_gate, ffn_w_up, ffn_w_down, moe_router, moe_w_gate, moe_w_up, moe_w_down, ln_gain, ln_bias):
    batch, seq, d = x.shape
    depth = ln_gain.shape[0]
    alpha = float((2 * depth) ** 0.25)
    cos, sin = _rope_tables(seq)
    xf = x.reshape(batch * seq, d)
    xbf = xf
    for i in range(depth):
        j = i // 2
        if i % 2 == 0:
            q, k2, v2 = _proj_a(xbf, _prep_a_w_in(a_w_in[j]), cos, sin, seq)
            o = _swa_attention(q, k2, v2, a_sinks[j], batch, seq)
            xf, xbf = _outproj_ln(o, a_w_out[j].astype(BF16), xf, ln_gain[i, 0], ln_bias[i, 0], alpha)
            xf, xbf = _ffn_ln(xbf, xf, ffn_w_gate[j].astype(BF16), ffn_w_up[j].astype(BF16),
                              ffn_w_down[j].astype(BF16), ln_gain[i, 1], ln_bias[i, 1], alpha)
        else:
            q, kc, ks2, kw2, vc, vs2, vw2, gates = _proj_b(xbf, _prep_b_w_in(b_w_in[j]), cos, sin, seq)
            pk, w1k, w2k = _prep_compress(b_cmp_pos_k[j], b_cmp_k_w1[j], b_cmp_k_w2[j], True)
            pv, w1v, w2v = _prep_compress(b_cmp_pos_v[j], b_cmp_v_w1[j], b_cmp_v_w2[j], False)
            kcc, vcc = _compress(kc, vc, pk, pv, w1k, w1v, w2k, w2v, batch, seq)
            ovt = _overlap_t(seq // CMP_STRIDE, seq // SEL_BLOCK)
            o = _nsa_attention(q, gates, kcc, vcc, ks2, vs2, kw2, vw2, ovt,
                               _expand(seq // SEL_BLOCK, seq), batch, seq)
            xf, xbf, route = _outproj_ln(o, b_w_out[j].astype(BF16), xf, ln_gain[i, 0], ln_bias[i, 0], alpha,
                                         router_t=moe_router[j].T)
            xf, xbf = _moe(xf, xbf, route, moe_w_gate[j].astype(BF16), moe_w_up[j].astype(BF16),
                           moe_w_down[j].astype(BF16), ln_gain[i, 1], ln_bias[i, 1], alpha)
    return xf.reshape(batch, seq, d)
```

```python
import functools
import math

import numpy as np
import jax
import jax.numpy as jnp
from jax import lax
from jax.experimental import pallas as pl
from jax.experimental.pallas import tpu as pltpu

F32 = jnp.float32
BF16 = jnp.bfloat16

HEAD_DIM = 64
HALF_DIM = HEAD_DIM // 2
LANES = 128
ROPE_THETA = 10000.0
KV_HEADS = 4
GROUP = 4
SWA_BLOCK = 128
SWA_WINDOW = 128
CMP_LEN = 32
CMP_STRIDE = 16
SEL_BLOCK = 64
SEL_SHIFT = 6
SEL_TOPN = 16
NSA_WINDOW = 512
SEL_FORCE = 1e4
N_GATES = 3
LN_EPS = 1e-5
NEG_INF = -1e30
SEL_MASK = 2.0 ** 100
VMEM_LIMIT = 48 * 1024 * 1024

NSA_TQ = 128
NSA_KC = 512


def _cparams(*sem):
    return pltpu.CompilerParams(dimension_semantics=sem, vmem_limit_bytes=VMEM_LIMIT)


def _dot(a, b):
    return jnp.dot(a, b, preferred_element_type=F32)


def _dot_nt(a, b):
    return lax.dot_general(a, b, (((1,), (1,)), ((), ())), preferred_element_type=F32)


def _dot_tn(a, b):
    return lax.dot_general(a, b, (((0,), (0,)), ((), ())), preferred_element_type=F32)


def _pair_cols(n_heads):
    idx = []
    for j in range(n_heads // 2):
        a, b = 2 * j * HEAD_DIM, (2 * j + 1) * HEAD_DIM
        idx += list(range(a, a + HALF_DIM)) + list(range(b, b + HALF_DIM))
        idx += list(range(a + HALF_DIM, a + HEAD_DIM)) + list(range(b + HALF_DIM, b + HEAD_DIM))
    return np.asarray(idx, np.int32)


def _kdup_cols(n_heads):
    idx = []
    for h in range(n_heads):
        a = h * HEAD_DIM
        idx += list(range(a, a + HALF_DIM)) * 2 + list(range(a + HALF_DIM, a + HEAD_DIM)) * 2
    return np.asarray(idx, np.int32)


def _vdup_cols(n_heads):
    idx = []
    for h in range(n_heads):
        idx += list(range(h * HEAD_DIM, (h + 1) * HEAD_DIM)) * 2
    return np.asarray(idx, np.int32)


def _rope_tables(seq):
    inv = 1.0 / (ROPE_THETA ** (jnp.arange(0, HEAD_DIM, 2, dtype=F32) / HEAD_DIM))
    ang = jnp.arange(seq, dtype=F32)[:, None] * inv[None, :]
    cos, sin = jnp.cos(ang), jnp.sin(ang)
    return jnp.tile(cos, (1, 4)), jnp.concatenate([-sin, -sin, sin, sin], axis=1)


def _rope(y, c, s):
    return y * c + pltpu.roll(y, LANES // 2, 1) * s


def _layer_norm(z, g, b):
    mu = jnp.mean(z, axis=-1, keepdims=True)
    zc = z - mu
    var = jnp.mean(zc * zc, axis=-1, keepdims=True)
    return zc * lax.rsqrt(var + LN_EPS) * g + b


def _proj_a_kernel(x_ref, w_ref, cos_ref, sin_ref, q_ref, k_ref, v_ref):
    x = x_ref[...].astype(BF16)
    c, s = cos_ref[...], sin_ref[...]
    nq = q_ref.shape[1]
    nk = k_ref.shape[1]
    for j in range(nq // LANES):
        y = _dot(x, w_ref[:, j * LANES:(j + 1) * LANES])
        q_ref[:, j * LANES:(j + 1) * LANES] = _rope(y, c, s).astype(BF16)
    for j in range(nk // LANES):
        y = _dot(x, w_ref[:, nq + j * LANES:nq + (j + 1) * LANES])
        k_ref[:, j * LANES:(j + 1) * LANES] = _rope(y, c, s).astype(BF16)
    v_ref[...] = _dot(x, w_ref[:, nq + nk:]).astype(BF16)


def _proj_a(x2d, w, cos, sin, seq, tm=512):
    t, d = x2d.shape
    nq = d
    nk = KV_HEADS * LANES
    per_seq = seq // tm
    return pl.pallas_call(
        _proj_a_kernel,
        grid=(t // tm,),
        in_specs=[
            pl.BlockSpec((tm, d), lambda i: (i, 0)),
            pl.BlockSpec(w.shape, lambda i: (0, 0)),
            pl.BlockSpec((tm, LANES), lambda i: (i % per_seq, 0)),
            pl.BlockSpec((tm, LANES), lambda i: (i % per_seq, 0)),
        ],
        out_specs=[
            pl.BlockSpec((tm, nq), lambda i: (i, 0)),
            pl.BlockSpec((tm, nk), lambda i: (i, 0)),
            pl.BlockSpec((tm, nk), lambda i: (i, 0)),
        ],
        out_shape=[
            jax.ShapeDtypeStruct((t, nq), BF16),
            jax.ShapeDtypeStruct((t, nk), BF16),
            jax.ShapeDtypeStruct((t, nk), BF16),
        ],
        compiler_params=_cparams("parallel"),
        name="proj_a",
    )(x2d, w, cos, sin)


def _proj_b_kernel(x_ref, w_ref, cos_ref, sin_ref,
                   q_ref, kc_ref, ks_ref, kw_ref, vc_ref, vs_ref, vw_ref, g_ref):
    x = x_ref[...].astype(BF16)
    c, s = cos_ref[...], sin_ref[...]
    col = 0
    for ref in (q_ref, kc_ref, ks_ref, kw_ref):
        for j in range(ref.shape[1] // LANES):
            y = _dot(x, w_ref[:, col:col + LANES])
            ref[:, j * LANES:(j + 1) * LANES] = _rope(y, c, s).astype(ref.dtype)
            col += LANES
    for ref in (vc_ref, vs_ref, vw_ref):
        n = ref.shape[1]
        ref[...] = _dot(x, w_ref[:, col:col + n]).astype(ref.dtype)
        col += n
    g_ref[...] = jax.nn.sigmoid(_dot(x, w_ref[:, col:]))


def _proj_b(x2d, w, cos, sin, seq, tm=512):
    t, d = x2d.shape
    nkv = KV_HEADS * HEAD_DIM
    ndup = KV_HEADS * LANES
    widths = [(d, BF16), (nkv, F32), (ndup, BF16), (ndup, BF16),
              (nkv, F32), (ndup, BF16), (ndup, BF16), (KV_HEADS * LANES, F32)]
    per_seq = seq // tm
    return pl.pallas_call(
        _proj_b_kernel,
        grid=(t // tm,),
        in_specs=[
            pl.BlockSpec((tm, d), lambda i: (i, 0)),
            pl.BlockSpec(w.shape, lambda i: (0, 0)),
            pl.BlockSpec((tm, LANES), lambda i: (i % per_seq, 0)),
            pl.BlockSpec((tm, LANES), lambda i: (i % per_seq, 0)),
        ],
        out_specs=[pl.BlockSpec((tm, n), lambda i: (i, 0)) for n, _ in widths],
        out_shape=[jax.ShapeDtypeStruct((t, n), dt) for n, dt in widths],
        compiler_params=_cparams("parallel"),
        name="proj_b",
    )(x2d, w, cos, sin)


def _stack_heads(q_ref, rows):
    lane = lax.broadcasted_iota(jnp.int32, (rows, LANES), 1)
    first = (lane & (HEAD_DIM - 1)) < HALF_DIM
    zero = jnp.zeros((rows, LANES), BF16)
    parts = []
    for p in range(GROUP // 2):
        qp = q_ref[:, p * LANES:(p + 1) * LANES]
        parts += [jnp.where(first, qp, zero), jnp.where(first, zero, qp)]
    return jnp.concatenate(parts, axis=0)


def _unstack_heads(o, rows):
    lane = lax.broadcasted_iota(jnp.int32, (rows, LANES), 1)
    low = lane < HEAD_DIM
    return [jnp.where(low, o[2 * p], o[2 * p + 1]) for p in range(GROUP // 2)]


def _swa_kernel(sink_ref, q_ref, kp_ref, kc_ref, vp_ref, vc_ref, o_ref):
    n = pl.program_id(1)
    blk = SWA_BLOCK
    qi = lax.broadcasted_iota(jnp.int32, (blk, 2 * blk), 0)
    si = lax.broadcasted_iota(jnp.int32, (blk, 2 * blk), 1)
    diff = qi + blk - si
    ok = (diff >= 0) & (diff < SWA_WINDOW) & ((si >= blk) | (n > 0))
    for g in range(KV_HEADS):
        q4 = _stack_heads(q_ref.at[:, g * 2 * LANES:(g + 1) * 2 * LANES], blk)
        kk = jnp.concatenate([kp_ref[:, g * LANES:(g + 1) * LANES], kc_ref[:, g * LANES:(g + 1) * LANES]], axis=0)
        vv = jnp.concatenate([vp_ref[:, g * LANES:(g + 1) * LANES], vc_ref[:, g * LANES:(g + 1) * LANES]], axis=0)
        s = _dot_nt(q4, kk).reshape(GROUP, blk, 2 * blk)
        s = jnp.where(ok[None], s, NEG_INF)
        ps, rs = [], []
        for h in range(GROUP):
            sink = sink_ref[g * GROUP + h]
            m = jnp.maximum(jnp.max(s[h], axis=-1, keepdims=True), sink)
            p = jnp.exp(s[h] - m)
            rs.append(1.0 / (jnp.sum(p, axis=-1, keepdims=True) + jnp.exp(sink - m)))
            ps.append(p.astype(BF16))
        o = _dot(jnp.concatenate(ps, axis=0), vv).reshape(GROUP, blk, LANES)
        o = [o[h] * rs[h] for h in range(GROUP)]
        for p, blkout in enumerate(_unstack_heads(o, blk)):
            c0 = (g * 2 + p) * LANES
            o_ref[:, c0:c0 + LANES] = blkout.astype(BF16)


def _swa_attention(q, k2, v2, sinks, batch, seq):
    t, nq = q.shape
    nk = k2.shape[1]
    nb = seq // SWA_BLOCK
    cur = lambda b, n: (b * nb + n, 0)
    prev = lambda b, n: (b * nb + jnp.maximum(n - 1, 0), 0)
    return pl.pallas_call(
        _swa_kernel,
        grid=(batch, nb),
        in_specs=[
            pl.BlockSpec(memory_space=pltpu.SMEM),
            pl.BlockSpec((SWA_BLOCK, nq), cur),
            pl.BlockSpec((SWA_BLOCK, nk), prev),
            pl.BlockSpec((SWA_BLOCK, nk), cur),
            pl.BlockSpec((SWA_BLOCK, nk), prev),
            pl.BlockSpec((SWA_BLOCK, nk), cur),
        ],
        out_specs=pl.BlockSpec((SWA_BLOCK, nq), cur),
        out_shape=jax.ShapeDtypeStruct((t, nq), BF16),
        compiler_params=_cparams("parallel", "parallel"),
        name="swa_attention",
    )(sinks, q, k2, k2, v2, v2)


def _outproj_ln_kernel(alpha, o_ref, w_ref, x_ref, g_ref, b_ref, y_ref, ybf_ref):
    z = alpha * x_ref[...].astype(F32) + _dot(o_ref[...], w_ref[...])
    y = _layer_norm(z, g_ref[...], b_ref[...])
    y_ref[...] = y
    ybf_ref[...] = y.astype(BF16)


def _outproj_ln_router_kernel(alpha, o_ref, w_ref, x_ref, g_ref, b_ref, r_ref, y_ref, ybf_ref, route_ref):
    z = alpha * x_ref[...].astype(F32) + _dot(o_ref[...], w_ref[...])
    y = _layer_norm(z, g_ref[...], b_ref[...])
    y_ref[...] = y
    ybf_ref[...] = y.astype(BF16)
    logits = lax.dot_general(r_ref[...], y, (((1,), (1,)), ((), ())),
                             precision=lax.Precision.HIGHEST, preferred_element_type=F32)
    ne = logits.shape[0]
    eid = lax.broadcasted_iota(jnp.int32, logits.shape, 0)
    m1 = jnp.max(logits, axis=0, keepdims=True)
    i1 = jnp.min(jnp.where(logits == m1, eid, ne), axis=0, keepdims=True)
    rest = jnp.where(eid == i1, -jnp.inf, logits)
    m2 = jnp.max(rest, axis=0, keepdims=True)
    i2 = jnp.min(jnp.where(rest == m2, eid, ne), axis=0, keepdims=True)
    e = jnp.exp(m2 - m1)
    w1 = 1.0 / (1.0 + e)
    w2 = e / (1.0 + e)
    route_ref[...] = jnp.concatenate(
        [i1.astype(F32), i2.astype(F32), w1, w2, jnp.zeros((4, logits.shape[1]), F32)], axis=0)


def _outproj_ln(o, w, x, gain, bias, alpha, router_t=None, tm=512):
    t, d = x.shape
    row = lambda i: (i, 0)
    const = lambda i: (0, 0)
    in_specs = [
        pl.BlockSpec((tm, o.shape[1]), row),
        pl.BlockSpec(w.shape, const),
        pl.BlockSpec((tm, d), row),
        pl.BlockSpec((1, d), const),
        pl.BlockSpec((1, d), const),
    ]
    out_specs = [pl.BlockSpec((tm, d), row), pl.BlockSpec((tm, d), row)]
    out_shape = [jax.ShapeDtypeStruct((t, d), F32), jax.ShapeDtypeStruct((t, d), BF16)]
    args = [o, w, x, gain.reshape(1, d), bias.reshape(1, d)]
    if router_t is None:
        body = functools.partial(_outproj_ln_kernel, alpha)
        name = "outproj_ln"
    else:
        body = functools.partial(_outproj_ln_router_kernel, alpha)
        name = "outproj_ln_router"
        in_specs.append(pl.BlockSpec(router_t.shape, const))
        args.append(router_t)
        out_specs.append(pl.BlockSpec((8, tm), lambda i: (0, i)))
        out_shape.append(jax.ShapeDtypeStruct((8, t), F32))
    return pl.pallas_call(
        body,
        grid=(t // tm,),
        in_specs=in_specs,
        out_specs=out_specs,
        out_shape=out_shape,
        compiler_params=_cparams("parallel"),
        name=name,
    )(*args)


def _ffn_ln_kernel(alpha, xbf_ref, x_ref, wg_ref, wu_ref, wd_ref, g_ref, b_ref, y_ref, ybf_ref, acc_ref):
    j = pl.program_id(1)

    @pl.when(j == 0)
    def _():
        acc_ref[...] = jnp.zeros_like(acc_ref)

    xb = xbf_ref[...]
    h = jax.nn.silu(_dot(xb, wg_ref[...])) * _dot(xb, wu_ref[...])
    acc_ref[...] += _dot(h.astype(BF16), wd_ref[...])

    @pl.when(j == pl.num_programs(1) - 1)
    def _():
        y = _layer_norm(alpha * x_ref[...] + acc_ref[...], g_ref[...], b_ref[...])
        y_ref[...] = y
        ybf_ref[...] = y.astype(BF16)


def _ffn_ln(xbf, x, wg, wu, wd, gain, bias, alpha, tm=512, tf=512):
    t, d = x.shape
    dff = wg.shape[1]
    row = lambda i, j: (i, 0)
    const = lambda i, j: (0, 0)
    return pl.pallas_call(
        functools.partial(_ffn_ln_kernel, alpha),
        grid=(t // tm, dff // tf),
        in_specs=[
            pl.BlockSpec((tm, d), row),
            pl.BlockSpec((tm, d), row),
            pl.BlockSpec((d, tf), lambda i, j: (0, j)),
            pl.BlockSpec((d, tf), lambda i, j: (0, j)),
            pl.BlockSpec((tf, d), lambda i, j: (j, 0)),
            pl.BlockSpec((1, d), const),
            pl.BlockSpec((1, d), const),
        ],
        out_specs=[pl.BlockSpec((tm, d), row), pl.BlockSpec((tm, d), row)],
        out_shape=[jax.ShapeDtypeStruct((t, d), F32), jax.ShapeDtypeStruct((t, d), BF16)],
        scratch_shapes=[pltpu.VMEM((tm, d), F32)],
        compiler_params=_cparams("parallel", "arbitrary"),
        name="ffn_ln",
    )(xbf, x, wg, wu, wd, gain.reshape(1, d), bias.reshape(1, d))


def _gelu_tanh(x):
    return 0.5 * x * (1.0 + jnp.tanh(math.sqrt(2.0 / math.pi) * (x + 0.044715 * (x * x * x))))


def _compress_kernel(kc_ref, vc_ref, pk_ref, pv_ref, w1k_ref, w1v_ref, w2k_ref, w2v_ref, ko_ref, vo_ref):
    nseg = ko_ref.shape[0]
    half = CMP_LEN // 2

    def one(t_ref, pos_ref, w1_ref, w2_ref, out_ref):
        hidden = w1_ref.shape[2]
        lo = jnp.zeros((nseg, hidden), F32)
        hi = jnp.zeros((nseg, hidden), F32)
        for l in range(half):
            rows = t_ref[pl.ds(l, nseg, stride=CMP_STRIDE), :]
            lo += _dot((rows + pos_ref[l:l + 1, :]).astype(BF16), w1_ref[l])
            hi += _dot((rows + pos_ref[half + l:half + l + 1, :]).astype(BF16), w1_ref[half + l])
        h = _gelu_tanh(lo + pltpu.roll(hi, nseg - 1, 0))
        out_ref[...] = _dot(h.astype(BF16), w2_ref[...]).astype(out_ref.dtype)

    one(kc_ref, pk_ref, w1k_ref, w2k_ref, ko_ref)
    one(vc_ref, pv_ref, w1v_ref, w2v_ref, vo_ref)


def _compress(kc, vc, pk, pv, w1k, w1v, w2k, w2v, batch, seq):
    nseg = seq // CMP_STRIDE
    npair = KV_HEADS // 2
    tok = lambda b, p: (b, p)
    c2 = lambda b, p: (0, 0)
    c3 = lambda b, p: (0, 0, 0)
    return pl.pallas_call(
        _compress_kernel,
        grid=(batch, npair),
        in_specs=[
            pl.BlockSpec((seq, LANES), tok),
            pl.BlockSpec((seq, LANES), tok),
            pl.BlockSpec(pk.shape, c2),
            pl.BlockSpec(pv.shape, c2),
            pl.BlockSpec(w1k.shape, c3),
            pl.BlockSpec(w1v.shape, c3),
            pl.BlockSpec(w2k.shape, c2),
            pl.BlockSpec(w2v.shape, c2),
        ],
        out_specs=[pl.BlockSpec((nseg, 2 * LANES), tok), pl.BlockSpec((nseg, 2 * LANES), tok)],
        out_shape=[jax.ShapeDtypeStruct((batch * nseg, KV_HEADS * LANES), BF16)] * 2,
        compiler_params=_cparams("parallel", "parallel"),
        name="nsa_compress",
    )(kc, vc, pk, pv, w1k, w1v, w2k, w2v)


def _lane_fold(x, op):
    out = x[:, 0:LANES]
    for j in range(1, x.shape[1] // LANES):
        out = op(out, x[:, j * LANES:(j + 1) * LANES])
    return out


def _nsa_kernel(q_ref, gate_ref, kcc_ref, vcc_ref, ks_ref, vs_ref, kw_ref, vw_ref, ovt_ref, blk_ref,
                o_ref, qa_ref, imp_ref, s_ref, mx_ref, l_ref, acc_ref):
    tq, kc = NSA_TQ, NSA_KC
    rows = GROUP * tq
    ncp = kcc_ref.shape[0]
    nsel = ovt_ref.shape[0]
    topn = min(SEL_TOPN, nsel)
    t0 = pl.program_id(2) * tq
    qa_ref[:, 0:LANES] = _stack_heads(q_ref, tq)
    tpos = t0 + lax.broadcasted_iota(jnp.int32, (tq, 1), 0)


    cend = CMP_STRIDE * lax.broadcasted_iota(jnp.int32, (1, ncp), 1) + (CMP_LEN - 1)
    ok_c = (cend <= tpos)[None]
    s = jnp.where(ok_c, _dot_nt(qa_ref[:, 0:LANES], kcc_ref[...]).reshape(GROUP, tq, ncp), NEG_INF)
    p = jnp.where(ok_c, jnp.exp(s - jnp.max(s, axis=-1, keepdims=True)), 0.0)
    p = p * (1.0 / jnp.maximum(jnp.sum(p, axis=-1, keepdims=True), 1e-30))
    o_c = _dot(p.reshape(rows, ncp).astype(BF16), vcc_ref[...])
    psum = p[0] + p[1] + p[2] + p[3]

    ovt = ovt_ref[...]
    imp = jnp.zeros((nsel, tq), F32)
    rem = psum
    for _ in range(3):
        part = rem.astype(BF16)
        imp += _dot_nt(ovt, part)
        rem = rem - part.astype(F32)
    blk = lax.broadcasted_iota(jnp.int32, (nsel, tq), 0)
    cur = jnp.right_shift(t0 + lax.broadcasted_iota(jnp.int32, (nsel, tq), 1), SEL_SHIFT)
    forced = (blk == 0) | (blk == cur) | (blk == cur - 1)
    imp = jnp.where(forced, SEL_FORCE, imp)
    imp = jnp.where(blk <= cur, imp, -1.0)
    imp_ref[...] = imp

    ngrp = nsel // 8
    sub = lax.broadcasted_iota(jnp.int32, (8, tq), 0)
    grp = [imp[8 * r:8 * r + 8] for r in range(ngrp)]
    cnt = [jnp.zeros((8, tq), F32) for _ in range(ngrp)]
    for i in range(nsel):
        row = jnp.broadcast_to(imp_ref[i:i + 1, :], (8, tq))
        for r in range(ngrp):
            ge = jnp.where(row >= grp[r], 1.0, 0.0)
            gt = jnp.where(row > grp[r], 1.0, 0.0)
            if 8 * r > i:
                cnt[r] = cnt[r] + ge
            elif 8 * r + 7 < i:
                cnt[r] = cnt[r] + gt
            else:
                cnt[r] = cnt[r] + jnp.where(sub > i - 8 * r, ge, gt)
    rank = jnp.concatenate(cnt, axis=0)
    unsel = jnp.where((rank < topn) & (blk <= cur), 0.0, -1.0)
    if nsel < LANES:
        unsel = jnp.concatenate([unsel, jnp.zeros((LANES - nsel, tq), F32)], axis=0)
    selq = unsel.T.astype(BF16)
    for h in range(GROUP):
        qa_ref[h * tq:(h + 1) * tq, LANES:2 * LANES] = selq

    def key_side(start):
        return jnp.concatenate([ks_ref[pl.ds(start, kc), :], blk_ref[pl.ds(start, kc), :]], axis=1)

    mx_ref[...] = jnp.full(mx_ref.shape, NEG_INF, F32)

    def score_chunk(c, carry):
        s = _dot_nt(qa_ref[...], key_side(pl.multiple_of(c * kc, kc)))
        s_ref[c] = s
        mx_ref[...] = jnp.maximum(mx_ref[...], _lane_fold(s, jnp.maximum))
        return carry

    nfull = t0 // kc
    lax.fori_loop(0, nfull, score_chunk, 0)
    dstart = pl.multiple_of(nfull * kc, kc)
    causal = jnp.where(dstart + lax.broadcasted_iota(jnp.int32, (1, kc), 1) <= tpos, 0.0, NEG_INF)
    s = (_dot_nt(qa_ref[...], key_side(dstart)).reshape(GROUP, tq, kc) + causal[None]).reshape(rows, kc)
    s_ref[nfull] = s
    m = jnp.max(jnp.maximum(mx_ref[...], _lane_fold(s, jnp.maximum)), axis=-1, keepdims=True)
    mx_ref[...] = jnp.broadcast_to(m, (rows, LANES))

    l_ref[...] = jnp.zeros(l_ref.shape, F32)
    acc_ref[...] = jnp.zeros(acc_ref.shape, F32)

    def value_chunk(c, carry):
        start = pl.multiple_of(c * kc, kc)
        mrep = mx_ref[...]
        sc = s_ref[c]
        p = [jnp.exp(sc[:, j * LANES:(j + 1) * LANES] - mrep) for j in range(kc // LANES)]
        l_ref[...] += functools.reduce(jnp.add, p)
        pb = jnp.concatenate([x.astype(BF16) for x in p], axis=1)
        acc_ref[...] += _dot(pb, vs_ref[pl.ds(start, kc), :])
        return carry

    lax.fori_loop(0, nfull + 1, value_chunk, 0)

    span = NSA_WINDOW + tq
    w0 = pl.multiple_of(jnp.maximum(t0 - NSA_WINDOW, 0), LANES)
    d = tpos - (w0 + lax.broadcasted_iota(jnp.int32, (1, span), 1))
    bias = jnp.where((d >= 0) & (d < NSA_WINDOW), 0.0, NEG_INF)
    s = _dot_nt(qa_ref[:, 0:LANES], kw_ref[pl.ds(w0, span), :])
    s = (s.reshape(GROUP, tq, span) + bias[None]).reshape(rows, span)
    mw = jnp.broadcast_to(jnp.max(_lane_fold(s, jnp.maximum), axis=-1, keepdims=True), (rows, LANES))
    p = [jnp.exp(s[:, j * LANES:(j + 1) * LANES] - mw) for j in range(span // LANES)]
    lw = jnp.sum(functools.reduce(jnp.add, p), axis=-1, keepdims=True)
    accw = _dot(jnp.concatenate([x.astype(BF16) for x in p], axis=1), vw_ref[pl.ds(w0, span), :])

    gates = gate_ref[...]
    ls = jnp.sum(l_ref[...], axis=-1, keepdims=True)
    outs = []
    for h in range(GROUP):
        hs = slice(h * tq, (h + 1) * tq)
        gc = gates[:, N_GATES * h:N_GATES * h + 1]
        gs = gates[:, N_GATES * h + 1:N_GATES * h + 2]
        gw = gates[:, N_GATES * h + 2:N_GATES * h + 3]
        outs.append(gc * o_c[hs] + (gs / ls[hs]) * acc_ref[hs, :] + (gw / lw[hs]) * accw[hs])
    for pidx, blkout in enumerate(_unstack_heads(outs, tq)):
        o_ref[:, pidx * LANES:(pidx + 1) * LANES] = blkout.astype(BF16)


def _nsa_attention(q, gates, kcc, vcc, ks2, vs2, kw2, vw2, ovt, key_blk, batch, seq):
    t, nq = q.shape
    tq = NSA_TQ
    rows = GROUP * tq
    nt = seq // tq
    ncp = seq // CMP_STRIDE
    nsel = seq // SEL_BLOCK
    qmap = lambda b, g, i: (b * nt + i, g)
    kvmap = lambda b, g, i: (b, g)
    return pl.pallas_call(
        _nsa_kernel,
        grid=(batch, KV_HEADS, nt),
        in_specs=[
            pl.BlockSpec((tq, 2 * LANES), qmap),
            pl.BlockSpec((tq, LANES), qmap),
            pl.BlockSpec((ncp, LANES), kvmap),
            pl.BlockSpec((ncp, LANES), kvmap),
            pl.BlockSpec((seq, LANES), kvmap),
            pl.BlockSpec((seq, LANES), kvmap),
            pl.BlockSpec((seq, LANES), kvmap),
            pl.BlockSpec((seq, LANES), kvmap),
            pl.BlockSpec(ovt.shape, lambda b, g, i: (0, 0)),
            pl.BlockSpec(key_blk.shape, lambda b, g, i: (0, 0)),
        ],
        out_specs=pl.BlockSpec((tq, 2 * LANES), qmap),
        out_shape=jax.ShapeDtypeStruct((t, nq), BF16),
        scratch_shapes=[
            pltpu.VMEM((rows, 2 * LANES), BF16),
            pltpu.VMEM((nsel, tq), F32),
            pltpu.VMEM((seq // NSA_KC, rows, NSA_KC), F32),
            pltpu.VMEM((rows, LANES), F32),
            pltpu.VMEM((rows, LANES), F32),
            pltpu.VMEM((rows, LANES), F32),
        ],
        compiler_params=_cparams("parallel", "parallel", "arbitrary"),
        name="nsa_attention",
    )(q, gates, kcc, vcc, ks2, vs2, kw2, vw2, ovt, key_blk)


def _moe_kernel(te_ref, nu_ref, xs_ref, wg_ref, wu_ref, wd_ref, ys_ref, acc_ref):
    i = pl.program_id(0)
    j = pl.program_id(1)

    @pl.when(i < nu_ref[0])
    def _():
        @pl.when(j == 0)
        def _():
            acc_ref[...] = jnp.zeros_like(acc_ref)

        xb = xs_ref[...]
        h = jax.nn.silu(_dot(xb, wg_ref[0])) * _dot(xb, wu_ref[0])
        acc_ref[...] += _dot(h.astype(BF16), wd_ref[0])

        @pl.when(j == pl.num_programs(1) - 1)
        def _():
            ys_ref[...] = acc_ref[...].astype(ys_ref.dtype)


def _moe_ffn(xs, wg, wu, wd, tile_expert, n_used, tm, tf=512):
    p, d = xs.shape
    dff = wg.shape[2]
    nj = dff // tf

    def rows(i, j, te, nu):
        return (jnp.minimum(i, nu[0] - 1), 0)

    def jj(i, j, nu):
        return jnp.where(i < nu[0], j, nj - 1)

    return pl.pallas_call(
        _moe_kernel,
        grid_spec=pltpu.PrefetchScalarGridSpec(
            num_scalar_prefetch=2,
            grid=(p // tm, nj),
            in_specs=[
                pl.BlockSpec((tm, d), rows),
                pl.BlockSpec((1, d, tf), lambda i, j, te, nu: (te[i], 0, jj(i, j, nu))),
                pl.BlockSpec((1, d, tf), lambda i, j, te, nu: (te[i], 0, jj(i, j, nu))),
                pl.BlockSpec((1, tf, d), lambda i, j, te, nu: (te[i], jj(i, j, nu), 0)),
            ],
            out_specs=pl.BlockSpec((tm, d), rows),
            scratch_shapes=[pltpu.VMEM((tm, d), F32)],
        ),
        out_shape=jax.ShapeDtypeStruct((p, d), BF16),
        compiler_params=_cparams("arbitrary", "arbitrary"),
        name="moe_ffn",
    )(tile_expert, n_used, xs, wg, wu, wd)


def _combine_ln_kernel(alpha, x_ref, y1_ref, y2_ref, w_ref, g_ref, b_ref, o_ref, obf_ref):
    w = w_ref[...]
    f = w[:, 0:1] * y1_ref[...].astype(F32) + w[:, 1:2] * y2_ref[...].astype(F32)
    y = _layer_norm(alpha * x_ref[...] + f, g_ref[...], b_ref[...])
    o_ref[...] = y
    obf_ref[...] = y.astype(BF16)


def _combine_ln(x, y1, y2, wcol, gain, bias, alpha, tm=512):
    t, d = x.shape
    row = lambda i: (i, 0)
    const = lambda i: (0, 0)
    return pl.pallas_call(
        functools.partial(_combine_ln_kernel, alpha),
        grid=(t // tm,),
        in_specs=[
            pl.BlockSpec((tm, d), row),
            pl.BlockSpec((tm, d), row),
            pl.BlockSpec((tm, d), row),
            pl.BlockSpec((tm, 2), row),
            pl.BlockSpec((1, d), const),
            pl.BlockSpec((1, d), const),
        ],
        out_specs=[pl.BlockSpec((tm, d), row), pl.BlockSpec((tm, d), row)],
        out_shape=[jax.ShapeDtypeStruct((t, d), F32), jax.ShapeDtypeStruct((t, d), BF16)],
        compiler_params=_cparams("parallel"),
        name="combine_ln",
    )(x, y1, y2, wcol, gain.reshape(1, d), bias.reshape(1, d))


def _moe(x, xbf, route, wg, wu, wd, gain, bias, alpha, tm=512):
    t, d = x.shape
    ne = wg.shape[0]
    ids = route[0:2].astype(jnp.int32).reshape(-1)
    tok = jnp.tile(jnp.arange(t, dtype=jnp.int32), 2)
    onehot = (ids[:, None] == jnp.arange(ne, dtype=jnp.int32)[None, :]).astype(jnp.int32)
    csum = jnp.cumsum(onehot, axis=0)
    rank = jnp.sum(onehot * (csum - 1), axis=1)
    counts = csum[-1]
    padded = ((counts + tm - 1) // tm) * tm
    ends = jnp.cumsum(padded)
    starts = ends - padded
    pos = jnp.sum(onehot * starts[None, :], axis=1) + rank
    n_rows = 2 * t + ne * tm
    n_tiles = n_rows // tm
    src = jnp.zeros((n_rows,), jnp.int32).at[pos].set(tok, unique_indices=True)
    tile_start = jnp.arange(n_tiles, dtype=jnp.int32) * tm
    tile_expert = jnp.minimum(jnp.sum((tile_start[:, None] >= ends[None, :]).astype(jnp.int32), axis=1), ne - 1)
    n_used = (ends[-1] // tm).astype(jnp.int32)
    last_expert = tile_expert[jnp.maximum(n_used - 1, 0)]
    tile_expert = jnp.where(jnp.arange(n_tiles) < n_used, tile_expert, last_expert).astype(jnp.int32)
    xs = jnp.take(xbf, src, axis=0)
    ys = _moe_ffn(xs, wg, wu, wd, tile_expert, n_used.reshape(1), tm)
    y1 = jnp.take(ys, pos[:t], axis=0)
    y2 = jnp.take(ys, pos[t:], axis=0)
    wcol = route[2:4].T
    return _combine_ln(x, y1, y2, wcol, gain, bias, alpha)


def _prep_a_w_in(w):
    d = w.shape[0]
    nq = d
    nkv = KV_HEADS * HEAD_DIM
    scale = HEAD_DIM ** -0.5
    q = w[:, :nq][:, _pair_cols(nq // HEAD_DIM)] * scale
    k = w[:, nq:nq + nkv][:, _kdup_cols(KV_HEADS)]
    v = w[:, nq + nkv:][:, _vdup_cols(KV_HEADS)]
    return jnp.concatenate([q, k, v], axis=1).astype(BF16)


def _prep_b_w_in(w):
    d = w.shape[0]
    nq = d
    nkv = KV_HEADS * HEAD_DIM
    scale = HEAD_DIM ** -0.5
    part = lambda i: w[:, nq + i * nkv:nq + (i + 1) * nkv]
    q = w[:, :nq][:, _pair_cols(nq // HEAD_DIM)] * scale
    kc = part(0)[:, _pair_cols(KV_HEADS)]
    vc = part(1)
    ks = part(2)[:, _kdup_cols(KV_HEADS)]
    vs = part(3)[:, _vdup_cols(KV_HEADS)]
    kw = part(4)[:, _kdup_cols(KV_HEADS)]
    vw = part(5)[:, _vdup_cols(KV_HEADS)]
    gl = w[:, nq + 6 * nkv:]
    per = GROUP * N_GATES
    gcols = []
    for g in range(KV_HEADS):
        gcols.append(jnp.pad(gl[:, g * per:(g + 1) * per], ((0, 0), (0, LANES - per))))
    return jnp.concatenate([q, kc, ks, kw, vc, vs, vw] + gcols, axis=1).astype(BF16)


def _prep_compress(pos, w1, w2, rope_layout):
    hidden = w1.shape[1]
    w1 = w1.reshape(CMP_LEN, HEAD_DIM, hidden)
    zeros = jnp.zeros_like(w1)
    a = jnp.concatenate([w1, zeros], axis=2)
    b = jnp.concatenate([zeros, w1], axis=2)
    natural = jnp.concatenate([a, b], axis=1)
    pos2 = jnp.concatenate([pos, pos], axis=1)
    if rope_layout:
        cols = _pair_cols(2)
        natural, pos2 = natural[:, cols, :], pos2[:, cols]
        out_cols = _kdup_cols(1)
    else:
        out_cols = _vdup_cols(1)
    w2d = w2[:, out_cols]
    z2 = jnp.zeros_like(w2d)
    w2p = jnp.concatenate([jnp.concatenate([w2d, z2], axis=1), jnp.concatenate([z2, w2d], axis=1)], axis=0)
    return pos2.astype(F32), natural.astype(BF16), w2p.astype(BF16)


def _overlap_t(ncp, nsel):
    cs = CMP_STRIDE * np.arange(ncp)
    ce = cs + CMP_LEN
    ss = SEL_BLOCK * np.arange(nsel)
    se = ss + SEL_BLOCK
    ov = np.clip(np.minimum(ce[None, :], se[:, None]) - np.maximum(cs[None, :], ss[:, None]), 0, None)
    return jnp.asarray(ov / CMP_STRIDE, dtype=BF16)


def _key_block_indicator(seq):
    assert seq // SEL_BLOCK <= LANES
    hit = (np.arange(seq) // SEL_BLOCK)[:, None] == np.arange(LANES)[None, :]
    return jnp.asarray(np.where(hit, SEL_MASK, 0.0), dtype=BF16)


def kernel(x, a_w_in, a_w_out, a_sinks, b_w_in, b_w_out, b_cmp_pos_k, b_cmp_pos_v, b_cmp_k_w1, b_cmp_k_w2, b_cmp_v_w1, b_cmp_v_w2, ffn_w_gate, ffn_w_up, ffn_w_down, moe_router, moe_w_gate, moe_w_up, moe_w_down, ln_gain, ln_bias):
    batch, seq, d = x.shape
    depth = ln_gain.shape[0]
    alpha = float((2 * depth) ** 0.25)
    cos, sin = _rope_tables(seq)
    xf = x.reshape(batch * seq, d)
    xbf = xf
    for i in range(depth):
        j = i // 2
        if i % 2 == 0:
            q, k2, v2 = _proj_a(xbf, _prep_a_w_in(a_w_in[j]), cos, sin, seq)
            o = _swa_attention(q, k2, v2, a_sinks[j], batch, seq)
            xf, xbf = _outproj_ln(o, a_w_out[j].astype(BF16), xf, ln_gain[i, 0], ln_bias[i, 0], alpha)
            xf, xbf = _ffn_ln(xbf, xf, ffn_w_gate[j].astype(BF16), ffn_w_up[j].astype(BF16),
                              ffn_w_down[j].astype(BF16), ln_gain[i, 1], ln_bias[i, 1], alpha)
        else:
            q, kc, ks2, kw2, vc, vs2, vw2, gates = _proj_b(xbf, _prep_b_w_in(b_w_in[j]), cos, sin, seq)
            pk, w1k, w2k = _prep_compress(b_cmp_pos_k[j], b_cmp_k_w1[j], b_cmp_k_w2[j], True)
            pv, w1v, w2v = _prep_compress(b_cmp_pos_v[j], b_cmp_v_w1[j], b_cmp_v_w2[j], False)
            kcc, vcc = _compress(kc, vc, pk, pv, w1k, w1v, w2k, w2v, batch, seq)
            ovt = _overlap_t(seq // CMP_STRIDE, seq // SEL_BLOCK)
            o = _nsa_attention(q, gates, kcc, vcc, ks2, vs2, kw2, vw2, ovt,
                               _key_block_indicator(seq), batch, seq)
            xf, xbf, route = _outproj_ln(o, b_w_out[j].astype(BF16), xf, ln_gain[i, 0], ln_bias[i, 0], alpha,
                                         router_t=moe_router[j].T)
            xf, xbf = _moe(xf, xbf, route, moe_w_gate[j].astype(BF16), moe_w_up[j].astype(BF16),
                           moe_w_down[j].astype(BF16), ln_gain[i, 1], ln_bias[i, 1], alpha)
    return xf.reshape(batch, seq, d)
```

```python
import functools
import math

import numpy as np
import jax
import jax.numpy as jnp
from jax import lax
from jax.experimental import pallas as pl
from jax.experimental.pallas import tpu as pltpu

F32 = jnp.float32
BF16 = jnp.bfloat16

HEAD_DIM = 64
HALF_DIM = HEAD_DIM // 2
LANES = 128
ROPE_THETA = 10000.0
LOG2E = math.log2(math.e)
Q_SCALE = HEAD_DIM ** -0.5 * LOG2E
KV_HEADS = 4
GROUP = 4
SWA_BLOCK = 128
SWA_WINDOW = 128
CMP_LEN = 32
CMP_STRIDE = 16
SEL_BLOCK = 64
SEL_SHIFT = 6
SEL_TOPN = 16
NSA_WINDOW = 512
SEL_FORCE = 1e4
N_GATES = 3
LN_EPS = 1e-5
NEG_INF = -1e30
SEL_MASK = 2.0 ** 100
VMEM_LIMIT = 48 * 1024 * 1024

NSA_TQ = 128
NSA_KC = 512


def _cparams(*sem):
    return pltpu.CompilerParams(dimension_semantics=sem, vmem_limit_bytes=VMEM_LIMIT)


def _dot(a, b):
    return jnp.dot(a, b, preferred_element_type=F32)


def _dot_nt(a, b):
    return lax.dot_general(a, b, (((1,), (1,)), ((), ())), preferred_element_type=F32)


def _dot_tn(a, b):
    return lax.dot_general(a, b, (((0,), (0,)), ((), ())), preferred_element_type=F32)


def _pair_cols(n_heads):
    idx = []
    for j in range(n_heads // 2):
        a, b = 2 * j * HEAD_DIM, (2 * j + 1) * HEAD_DIM
        idx += list(range(a, a + HALF_DIM)) + list(range(b, b + HALF_DIM))
        idx += list(range(a + HALF_DIM, a + HEAD_DIM)) + list(range(b + HALF_DIM, b + HEAD_DIM))
    return np.asarray(idx, np.int32)


def _kdup_cols(n_heads):
    idx = []
    for h in range(n_heads):
        a = h * HEAD_DIM
        idx += list(range(a, a + HALF_DIM)) * 2 + list(range(a + HALF_DIM, a + HEAD_DIM)) * 2
    return np.asarray(idx, np.int32)


def _vdup_cols(n_heads):
    idx = []
    for h in range(n_heads):
        idx += list(range(h * HEAD_DIM, (h + 1) * HEAD_DIM)) * 2
    return np.asarray(idx, np.int32)


def _rope_tables(seq):
    inv = 1.0 / (ROPE_THETA ** (jnp.arange(0, HEAD_DIM, 2, dtype=F32) / HEAD_DIM))
    ang = jnp.arange(seq, dtype=F32)[:, None] * inv[None, :]
    cos, sin = jnp.cos(ang), jnp.sin(ang)
    c, s = jnp.tile(cos, (1, 4)), jnp.concatenate([-sin, -sin, sin, sin], axis=1)
    return jnp.concatenate([c * Q_SCALE, s * Q_SCALE, c, s], axis=1)


def _rope(y, c, s):
    return y * c + pltpu.roll(y, LANES // 2, 1) * s


def _split_tables(tab_ref):
    tab = tab_ref[...]
    return [tab[:, i * LANES:(i + 1) * LANES] for i in range(4)]


def _layer_norm(z, g, b):
    mu = jnp.mean(z, axis=-1, keepdims=True)
    zc = z - mu
    var = jnp.mean(zc * zc, axis=-1, keepdims=True)
    return zc * lax.rsqrt(var + LN_EPS) * g + b


def _proj_a_kernel(x_ref, w_ref, tab_ref, q_ref, k_ref, v_ref):
    x = x_ref[...].astype(BF16)
    cq, sq, c, s = _split_tables(tab_ref)
    nq = q_ref.shape[1]
    nk = k_ref.shape[1]
    for j in range(nq // LANES):
        y = _dot(x, w_ref[:, j * LANES:(j + 1) * LANES])
        q_ref[:, j * LANES:(j + 1) * LANES] = _rope(y, cq, sq).astype(BF16)
    for j in range(nk // LANES):
        y = _dot(x, w_ref[:, nq + j * LANES:nq + (j + 1) * LANES])
        k_ref[:, j * LANES:(j + 1) * LANES] = _rope(y, c, s).astype(BF16)
    v_ref[...] = _dot(x, w_ref[:, nq + nk:]).astype(BF16)


def _proj_a(x2d, w, tables, seq, tm=512):
    t, d = x2d.shape
    nq = d
    nk = KV_HEADS * LANES
    per_seq = seq // tm
    return pl.pallas_call(
        _proj_a_kernel,
        grid=(t // tm,),
        in_specs=[
            pl.BlockSpec((tm, d), lambda i: (i, 0)),
            pl.BlockSpec(w.shape, lambda i: (0, 0)),
            pl.BlockSpec((tm, 4 * LANES), lambda i: (i % per_seq, 0)),
        ],
        out_specs=[
            pl.BlockSpec((tm, nq), lambda i: (i, 0)),
            pl.BlockSpec((tm, nk), lambda i: (i, 0)),
            pl.BlockSpec((tm, nk), lambda i: (i, 0)),
        ],
        out_shape=[
            jax.ShapeDtypeStruct((t, nq), BF16),
            jax.ShapeDtypeStruct((t, nk), BF16),
            jax.ShapeDtypeStruct((t, nk), BF16),
        ],
        compiler_params=_cparams("parallel"),
        name="proj_a",
    )(x2d, w, tables)


def _proj_b_kernel(x_ref, w_ref, tab_ref,
                   q_ref, kc_ref, ks_ref, kw_ref, vc_ref, vs_ref, vw_ref, g_ref):
    x = x_ref[...].astype(BF16)
    cq, sq, c, s = _split_tables(tab_ref)
    col = 0
    for ref in (q_ref, kc_ref, ks_ref, kw_ref):
        cr, sr = (cq, sq) if ref is q_ref else (c, s)
        for j in range(ref.shape[1] // LANES):
            y = _dot(x, w_ref[:, col:col + LANES])
            ref[:, j * LANES:(j + 1) * LANES] = _rope(y, cr, sr).astype(ref.dtype)
            col += LANES
    for ref in (vc_ref, vs_ref, vw_ref):
        n = ref.shape[1]
        ref[...] = _dot(x, w_ref[:, col:col + n]).astype(ref.dtype)
        col += n
    g_ref[...] = jax.nn.sigmoid(_dot(x, w_ref[:, col:]))


def _proj_b(x2d, w, tables, seq, tm=512):
    t, d = x2d.shape
    nkv = KV_HEADS * HEAD_DIM
    ndup = KV_HEADS * LANES
    widths = [(d, BF16), (nkv, F32), (ndup, BF16), (ndup, BF16),
              (nkv, F32), (ndup, BF16), (ndup, BF16), (KV_HEADS * LANES, F32)]
    per_seq = seq // tm
    return pl.pallas_call(
        _proj_b_kernel,
        grid=(t // tm,),
        in_specs=[
            pl.BlockSpec((tm, d), lambda i: (i, 0)),
            pl.BlockSpec(w.shape, lambda i: (0, 0)),
            pl.BlockSpec((tm, 4 * LANES), lambda i: (i % per_seq, 0)),
        ],
        out_specs=[pl.BlockSpec((tm, n), lambda i: (i, 0)) for n, _ in widths],
        out_shape=[jax.ShapeDtypeStruct((t, n), dt) for n, dt in widths],
        compiler_params=_cparams("parallel"),
        name="proj_b",
    )(x2d, w, tables)


def _stack_heads(q_ref, rows):
    lane = lax.broadcasted_iota(jnp.int32, (rows, LANES), 1)
    first = (lane & (HEAD_DIM - 1)) < HALF_DIM
    zero = jnp.zeros((rows, LANES), BF16)
    parts = []
    for p in range(GROUP // 2):
        qp = q_ref[:, p * LANES:(p + 1) * LANES]
        parts += [jnp.where(first, qp, zero), jnp.where(first, zero, qp)]
    return jnp.concatenate(parts, axis=0)


def _unstack_heads(o, rows):
    lane = lax.broadcasted_iota(jnp.int32, (rows, LANES), 1)
    low = lane < HEAD_DIM
    return [jnp.where(low, o[2 * p], o[2 * p + 1]) for p in range(GROUP // 2)]


def _swa_kernel(sink_ref, q_ref, kp_ref, kc_ref, vp_ref, vc_ref, o_ref):
    n = pl.program_id(1)
    blk = SWA_BLOCK
    qi = lax.broadcasted_iota(jnp.int32, (blk, 2 * blk), 0)
    si = lax.broadcasted_iota(jnp.int32, (blk, 2 * blk), 1)
    diff = qi + blk - si
    ok = (diff >= 0) & (diff < SWA_WINDOW) & ((si >= blk) | (n > 0))
    for g in range(KV_HEADS):
        q4 = _stack_heads(q_ref.at[:, g * 2 * LANES:(g + 1) * 2 * LANES], blk)
        kk = jnp.concatenate([kp_ref[:, g * LANES:(g + 1) * LANES], kc_ref[:, g * LANES:(g + 1) * LANES]], axis=0)
        vv = jnp.concatenate([vp_ref[:, g * LANES:(g + 1) * LANES], vc_ref[:, g * LANES:(g + 1) * LANES]], axis=0)
        s = _dot_nt(q4, kk).reshape(GROUP, blk, 2 * blk)
        s = jnp.where(ok[None], s, NEG_INF)
        ps, rs = [], []
        for h in range(GROUP):
            sink = sink_ref[g * GROUP + h] * LOG2E
            m = jnp.maximum(jnp.max(s[h], axis=-1, keepdims=True), sink)
            p = jnp.exp2(s[h] - m)
            rs.append(1.0 / (jnp.sum(p, axis=-1, keepdims=True) + jnp.exp2(sink - m)))
            ps.append(p.astype(BF16))
        o = _dot(jnp.concatenate(ps, axis=0), vv).reshape(GROUP, blk, LANES)
        o = [o[h] * rs[h] for h in range(GROUP)]
        for p, blkout in enumerate(_unstack_heads(o, blk)):
            c0 = (g * 2 + p) * LANES
            o_ref[:, c0:c0 + LANES] = blkout.astype(BF16)


def _swa_attention(q, k2, v2, sinks, batch, seq):
    t, nq = q.shape
    nk = k2.shape[1]
    nb = seq // SWA_BLOCK
    cur = lambda b, n: (b * nb + n, 0)
    prev = lambda b, n: (b * nb + jnp.maximum(n - 1, 0), 0)
    return pl.pallas_call(
        _swa_kernel,
        grid=(batch, nb),
        in_specs=[
            pl.BlockSpec(memory_space=pltpu.SMEM),
            pl.BlockSpec((SWA_BLOCK, nq), cur),
            pl.BlockSpec((SWA_BLOCK, nk), prev),
            pl.BlockSpec((SWA_BLOCK, nk), cur),
            pl.BlockSpec((SWA_BLOCK, nk), prev),
            pl.BlockSpec((SWA_BLOCK, nk), cur),
        ],
        out_specs=pl.BlockSpec((SWA_BLOCK, nq), cur),
        out_shape=jax.ShapeDtypeStruct((t, nq), BF16),
        compiler_params=_cparams("parallel", "parallel"),
        name="swa_attention",
    )(sinks, q, k2, k2, v2, v2)


def _outproj_ln_kernel(alpha, o_ref, w_ref, x_ref, g_ref, b_ref, y_ref, ybf_ref):
    z = alpha * x_ref[...].astype(F32) + _dot(o_ref[...], w_ref[...])
    y = _layer_norm(z, g_ref[...], b_ref[...])
    y_ref[...] = y
    ybf_ref[...] = y.astype(BF16)


def _outproj_ln_router_kernel(alpha, o_ref, w_ref, x_ref, g_ref, b_ref, r_ref, y_ref, ybf_ref, route_ref):
    z = alpha * x_ref[...].astype(F32) + _dot(o_ref[...], w_ref[...])
    y = _layer_norm(z, g_ref[...], b_ref[...])
    y_ref[...] = y
    ybf_ref[...] = y.astype(BF16)
    logits = lax.dot_general(r_ref[...], y, (((1,), (1,)), ((), ())),
                             precision=lax.Precision.HIGHEST, preferred_element_type=F32)
    ne = logits.shape[0]
    eid = lax.broadcasted_iota(jnp.int32, logits.shape, 0)
    m1 = jnp.max(logits, axis=0, keepdims=True)
    i1 = jnp.min(jnp.where(logits == m1, eid, ne), axis=0, keepdims=True)
    rest = jnp.where(eid == i1, -jnp.inf, logits)
    m2 = jnp.max(rest, axis=0, keepdims=True)
    i2 = jnp.min(jnp.where(rest == m2, eid, ne), axis=0, keepdims=True)
    e = jnp.exp(m2 - m1)
    w1 = 1.0 / (1.0 + e)
    w2 = e / (1.0 + e)
    route_ref[...] = jnp.concatenate(
        [i1.astype(F32), i2.astype(F32), w1, w2, jnp.zeros((4, logits.shape[1]), F32)], axis=0)


def _outproj_ln(o, w, x, gain, bias, alpha, router_t=None, tm=512):
    t, d = x.shape
    row = lambda i: (i, 0)
    const = lambda i: (0, 0)
    in_specs = [
        pl.BlockSpec((tm, o.shape[1]), row),
        pl.BlockSpec(w.shape, const),
        pl.BlockSpec((tm, d), row),
        pl.BlockSpec((1, d), const),
        pl.BlockSpec((1, d), const),
    ]
    out_specs = [pl.BlockSpec((tm, d), row), pl.BlockSpec((tm, d), row)]
    out_shape = [jax.ShapeDtypeStruct((t, d), F32), jax.ShapeDtypeStruct((t, d), BF16)]
    args = [o, w, x, gain.reshape(1, d), bias.reshape(1, d)]
    if router_t is None:
        body = functools.partial(_outproj_ln_kernel, alpha)
        name = "outproj_ln"
    else:
        body = functools.partial(_outproj_ln_router_kernel, alpha)
        name = "outproj_ln_router"
        in_specs.append(pl.BlockSpec(router_t.shape, const))
        args.append(router_t)
        out_specs.append(pl.BlockSpec((8, tm), lambda i: (0, i)))
        out_shape.append(jax.ShapeDtypeStruct((8, t), F32))
    return pl.pallas_call(
        body,
        grid=(t // tm,),
        in_specs=in_specs,
        out_specs=out_specs,
        out_shape=out_shape,
        compiler_params=_cparams("parallel"),
        name=name,
    )(*args)


def _ffn_ln_kernel(alpha, xbf_ref, x_ref, wg_ref, wu_ref, wd_ref, g_ref, b_ref, y_ref, ybf_ref, acc_ref):
    j = pl.program_id(1)

    @pl.when(j == 0)
    def _():
        acc_ref[...] = jnp.zeros_like(acc_ref)

    xb = xbf_ref[...]
    h = jax.nn.silu(_dot(xb, wg_ref[...])) * _dot(xb, wu_ref[...])
    acc_ref[...] += _dot(h.astype(BF16), wd_ref[...])

    @pl.when(j == pl.num_programs(1) - 1)
    def _():
        y = _layer_norm(alpha * x_ref[...] + acc_ref[...], g_ref[...], b_ref[...])
        y_ref[...] = y
        ybf_ref[...] = y.astype(BF16)


def _ffn_ln(xbf, x, wg, wu, wd, gain, bias, alpha, tm=512, tf=512):
    t, d = x.shape
    dff = wg.shape[1]
    row = lambda i, j: (i, 0)
    const = lambda i, j: (0, 0)
    return pl.pallas_call(
        functools.partial(_ffn_ln_kernel, alpha),
        grid=(t // tm, dff // tf),
        in_specs=[
            pl.BlockSpec((tm, d), row),
            pl.BlockSpec((tm, d), row),
            pl.BlockSpec((d, tf), lambda i, j: (0, j)),
            pl.BlockSpec((d, tf), lambda i, j: (0, j)),
            pl.BlockSpec((tf, d), lambda i, j: (j, 0)),
            pl.BlockSpec((1, d), const),
            pl.BlockSpec((1, d), const),
        ],
        out_specs=[pl.BlockSpec((tm, d), row), pl.BlockSpec((tm, d), row)],
        out_shape=[jax.ShapeDtypeStruct((t, d), F32), jax.ShapeDtypeStruct((t, d), BF16)],
        scratch_shapes=[pltpu.VMEM((tm, d), F32)],
        compiler_params=_cparams("parallel", "arbitrary"),
        name="ffn_ln",
    )(xbf, x, wg, wu, wd, gain.reshape(1, d), bias.reshape(1, d))


def _gelu_tanh(x):
    return 0.5 * x * (1.0 + jnp.tanh(math.sqrt(2.0 / math.pi) * (x + 0.044715 * (x * x * x))))


def _compress_kernel(kc_ref, vc_ref, pk_ref, pv_ref, w1k_ref, w1v_ref, w2k_ref, w2v_ref, ko_ref, vo_ref):
    nseg = ko_ref.shape[0]
    half = CMP_LEN // 2

    def one(t_ref, pos_ref, w1_ref, w2_ref, out_ref):
        hidden = w1_ref.shape[2]
        lo = jnp.zeros((nseg, hidden), F32)
        hi = jnp.zeros((nseg, hidden), F32)
        for l in range(half):
            rows = t_ref[pl.ds(l, nseg, stride=CMP_STRIDE), :]
            lo += _dot((rows + pos_ref[l:l + 1, :]).astype(BF16), w1_ref[l])
            hi += _dot((rows + pos_ref[half + l:half + l + 1, :]).astype(BF16), w1_ref[half + l])
        h = _gelu_tanh(lo + pltpu.roll(hi, nseg - 1, 0))
        out_ref[...] = _dot(h.astype(BF16), w2_ref[...]).astype(out_ref.dtype)

    one(kc_ref, pk_ref, w1k_ref, w2k_ref, ko_ref)
    one(vc_ref, pv_ref, w1v_ref, w2v_ref, vo_ref)


def _compress(kc, vc, pk, pv, w1k, w1v, w2k, w2v, batch, seq):
    nseg = seq // CMP_STRIDE
    npair = KV_HEADS // 2
    tok = lambda b, p: (b, p)
    c2 = lambda b, p: (0, 0)
    c3 = lambda b, p: (0, 0, 0)
    return pl.pallas_call(
        _compress_kernel,
        grid=(batch, npair),
        in_specs=[
            pl.BlockSpec((seq, LANES), tok),
            pl.BlockSpec((seq, LANES), tok),
            pl.BlockSpec(pk.shape, c2),
            pl.BlockSpec(pv.shape, c2),
            pl.BlockSpec(w1k.shape, c3),
            pl.BlockSpec(w1v.shape, c3),
            pl.BlockSpec(w2k.shape, c2),
            pl.BlockSpec(w2v.shape, c2),
        ],
        out_specs=[pl.BlockSpec((nseg, 2 * LANES), tok), pl.BlockSpec((nseg, 2 * LANES), tok)],
        out_shape=[jax.ShapeDtypeStruct((batch * nseg, KV_HEADS * LANES), BF16)] * 2,
        compiler_params=_cparams("parallel", "parallel"),
        name="nsa_compress",
    )(kc, vc, pk, pv, w1k, w1v, w2k, w2v)


def _lane_fold(x, op):
    out = x[:, 0:LANES]
    for j in range(1, x.shape[1] // LANES):
        out = op(out, x[:, j * LANES:(j + 1) * LANES])
    return out


def _nsa_kernel(q_ref, gate_ref, kcc_ref, vcc_ref, ks_ref, vs_ref, kw_ref, vw_ref, ovt_ref, blk_ref,
                o_ref, qa_ref, imp_ref, s_ref, mx_ref, l_ref, acc_ref, part_ref):
    tq, kc = NSA_TQ, NSA_KC
    rows = GROUP * tq
    ncp = kcc_ref.shape[0]
    nsel = ovt_ref.shape[0]
    topn = min(SEL_TOPN, nsel)
    t0 = pl.program_id(2) * tq
    qa_ref[:, 0:LANES] = _stack_heads(q_ref, tq)
    tpos = t0 + lax.broadcasted_iota(jnp.int32, (tq, 1), 0)
    gates = gate_ref[...]

    def gate_column(branch):
        return jnp.concatenate([gates[:, N_GATES * h + branch:N_GATES * h + branch + 1] for h in range(GROUP)],
                               axis=0)

    cend = CMP_STRIDE * lax.broadcasted_iota(jnp.int32, (1, ncp), 1) + (CMP_LEN - 1)
    ok_c = (cend <= tpos)[None]
    s = jnp.where(ok_c, _dot_nt(qa_ref[:, 0:LANES], kcc_ref[...]).reshape(GROUP, tq, ncp), NEG_INF)
    p = jnp.where(ok_c, jnp.exp2(s - jnp.max(s, axis=-1, keepdims=True)), 0.0)
    p = p * (1.0 / jnp.maximum(jnp.sum(p, axis=-1, keepdims=True), 1e-30))
    o_c = _dot(p.reshape(rows, ncp).astype(BF16), vcc_ref[...])
    psum = p[0] + p[1] + p[2] + p[3]

    span = NSA_WINDOW + tq
    w0 = pl.multiple_of(jnp.maximum(t0 - NSA_WINDOW, 0), LANES)
    d = tpos - (w0 + lax.broadcasted_iota(jnp.int32, (1, span), 1))
    bias = jnp.where((d >= 0) & (d < NSA_WINDOW), 0.0, NEG_INF)
    s = _dot_nt(qa_ref[:, 0:LANES], kw_ref[pl.ds(w0, span), :])
    s = (s.reshape(GROUP, tq, span) + bias[None]).reshape(rows, span)
    mw = jnp.broadcast_to(jnp.max(_lane_fold(s, jnp.maximum), axis=-1, keepdims=True), (rows, LANES))
    p = [jnp.exp2(s[:, j * LANES:(j + 1) * LANES] - mw) for j in range(span // LANES)]
    lw = jnp.sum(functools.reduce(jnp.add, p), axis=-1, keepdims=True)
    o_w = _dot(jnp.concatenate([x.astype(BF16) for x in p], axis=1), vw_ref[pl.ds(w0, span), :])
    part_ref[...] = gate_column(0) * o_c + (gate_column(2) / lw) * o_w

    ovt = ovt_ref[...]
    imp = jnp.zeros((nsel, tq), F32)
    rem = psum
    for _ in range(3):
        part = rem.astype(BF16)
        imp += _dot_nt(ovt, part)
        rem = rem - part.astype(F32)
    blk = lax.broadcasted_iota(jnp.int32, (nsel, tq), 0)
    cur = jnp.right_shift(t0 + lax.broadcasted_iota(jnp.int32, (nsel, tq), 1), SEL_SHIFT)
    forced = (blk == 0) | (blk == cur) | (blk == cur - 1)
    imp = jnp.where(forced, SEL_FORCE, imp)
    imp = jnp.where(blk <= cur, imp, -1.0)
    imp_ref[...] = imp

    ngrp = nsel // 8
    sub = lax.broadcasted_iota(jnp.int32, (8, tq), 0)
    grp = [imp[8 * r:8 * r + 8] for r in range(ngrp)]
    cnt = [jnp.zeros((8, tq), F32) for _ in range(ngrp)]
    for i in range(nsel):
        row = jnp.broadcast_to(imp_ref[i:i + 1, :], (8, tq))
        for r in range(ngrp):
            ge = jnp.where(row >= grp[r], 1.0, 0.0)
            gt = jnp.where(row > grp[r], 1.0, 0.0)
            if 8 * r > i:
                cnt[r] = cnt[r] + ge
            elif 8 * r + 7 < i:
                cnt[r] = cnt[r] + gt
            else:
                cnt[r] = cnt[r] + jnp.where(sub > i - 8 * r, ge, gt)
    rank = jnp.concatenate(cnt, axis=0)
    unsel = jnp.where((rank < topn) & (blk <= cur), 0.0, -1.0)
    if nsel < LANES:
        unsel = jnp.concatenate([unsel, jnp.zeros((LANES - nsel, tq), F32)], axis=0)
    selq = unsel.T.astype(BF16)
    for h in range(GROUP):
        qa_ref[h * tq:(h + 1) * tq, LANES:2 * LANES] = selq

    def key_side(start):
        return jnp.concatenate([ks_ref[pl.ds(start, kc), :], blk_ref[pl.ds(start, kc), :]], axis=1)

    nfull = t0 // kc
    dstart = pl.multiple_of(nfull * kc, kc)
    causal = jnp.where(dstart + lax.broadcasted_iota(jnp.int32, (1, kc), 1) <= tpos, 0.0, NEG_INF)
    s = (_dot_nt(qa_ref[...], key_side(dstart)).reshape(GROUP, tq, kc) + causal[None]).reshape(rows, kc)
    s_ref[nfull] = s
    mx_ref[...] = _lane_fold(s, jnp.maximum)

    def score_chunks(first, count):
        mx = mx_ref[...]
        for u in range(count):
            s = _dot_nt(qa_ref[...], key_side(pl.multiple_of((first + u) * kc, kc)))
            s_ref[first + u] = s
            mx = jnp.maximum(mx, _lane_fold(s, jnp.maximum))
        mx_ref[...] = mx

    def score_pair(c, carry):
        score_chunks(2 * c, 2)
        return carry

    lax.fori_loop(0, nfull // 2, score_pair, 0)

    @pl.when(nfull % 2 == 1)
    def _():
        score_chunks(nfull - 1, 1)

    mx_ref[...] = jnp.broadcast_to(jnp.max(mx_ref[...], axis=-1, keepdims=True), (rows, LANES))
    l_ref[...] = jnp.zeros(l_ref.shape, F32)
    acc_ref[...] = jnp.zeros(acc_ref.shape, F32)

    def value_chunks(first, count):
        mrep = mx_ref[...]
        lsum = l_ref[...]
        acc = acc_ref[...]
        for u in range(count):
            sc = s_ref[first + u]
            p = [jnp.exp2(sc[:, j * LANES:(j + 1) * LANES] - mrep) for j in range(kc // LANES)]
            lsum = lsum + functools.reduce(jnp.add, p)
            pb = jnp.concatenate([x.astype(BF16) for x in p], axis=1)
            acc = acc + _dot(pb, vs_ref[pl.ds(pl.multiple_of((first + u) * kc, kc), kc), :])
        l_ref[...] = lsum
        acc_ref[...] = acc

    def value_pair(c, carry):
        value_chunks(2 * c, 2)
        return carry

    lax.fori_loop(0, (nfull + 1) // 2, value_pair, 0)

    @pl.when(nfull % 2 == 0)
    def _():
        value_chunks(nfull, 1)

    ls = jnp.sum(l_ref[...], axis=-1, keepdims=True)
    out = part_ref[...] + (gate_column(1) / ls) * acc_ref[...]
    for pidx, blkout in enumerate(_unstack_heads(out.reshape(GROUP, tq, LANES), tq)):
        o_ref[:, pidx * LANES:(pidx + 1) * LANES] = blkout.astype(BF16)


def _nsa_attention(q, gates, kcc, vcc, ks2, vs2, kw2, vw2, ovt, key_blk, batch, seq):
    t, nq = q.shape
    tq = NSA_TQ
    rows = GROUP * tq
    nt = seq // tq
    ncp = seq // CMP_STRIDE
    nsel = seq // SEL_BLOCK
    qmap = lambda b, g, i: (b * nt + i, g)
    kvmap = lambda b, g, i: (b, g)
    return pl.pallas_call(
        _nsa_kernel,
        grid=(batch, KV_HEADS, nt),
        in_specs=[
            pl.BlockSpec((tq, 2 * LANES), qmap),
            pl.BlockSpec((tq, LANES), qmap),
            pl.BlockSpec((ncp, LANES), kvmap),
            pl.BlockSpec((ncp, LANES), kvmap),
            pl.BlockSpec((seq, LANES), kvmap),
            pl.BlockSpec((seq, LANES), kvmap),
            pl.BlockSpec((seq, LANES), kvmap),
            pl.BlockSpec((seq, LANES), kvmap),
            pl.BlockSpec(ovt.shape, lambda b, g, i: (0, 0)),
            pl.BlockSpec(key_blk.shape, lambda b, g, i: (0, 0)),
        ],
        out_specs=pl.BlockSpec((tq, 2 * LANES), qmap),
        out_shape=jax.ShapeDtypeStruct((t, nq), BF16),
        scratch_shapes=[
            pltpu.VMEM((rows, 2 * LANES), BF16),
            pltpu.VMEM((nsel, tq), F32),
            pltpu.VMEM((seq // NSA_KC, rows, NSA_KC), F32),
            pltpu.VMEM((rows, LANES), F32),
            pltpu.VMEM((rows, LANES), F32),
            pltpu.VMEM((rows, LANES), F32),
            pltpu.VMEM((rows, LANES), F32),
        ],
        compiler_params=_cparams("parallel", "parallel", "arbitrary"),
        name="nsa_attention",
    )(q, gates, kcc, vcc, ks2, vs2, kw2, vw2, ovt, key_blk)


def _moe_kernel(te_ref, nu_ref, xs_ref, wg_ref, wu_ref, wd_ref, ys_ref, acc_ref):
    i = pl.program_id(0)
    j = pl.program_id(1)

    @pl.when(i < nu_ref[0])
    def _():
        @pl.when(j == 0)
        def _():
            acc_ref[...] = jnp.zeros_like(acc_ref)

        xb = xs_ref[...]
        h = jax.nn.silu(_dot(xb, wg_ref[0])) * _dot(xb, wu_ref[0])
        acc_ref[...] += _dot(h.astype(BF16), wd_ref[0])

        @pl.when(j == pl.num_programs(1) - 1)
        def _():
            ys_ref[...] = acc_ref[...].astype(ys_ref.dtype)

    @pl.when((i >= nu_ref[0]) & (j == pl.num_programs(1) - 1))
    def _():
        ys_ref[...] = jnp.zeros_like(ys_ref)


def _moe_ffn(xs, wg, wu, wd, tile_expert, n_used, tm, tf=512):
    p, d = xs.shape
    dff = wg.shape[2]
    nj = dff // tf

    def rows(i, j, te, nu):
        return (jnp.minimum(i, nu[0] - 1), 0)

    def jj(i, j, nu):
        return jnp.where(i < nu[0], j, nj - 1)

    return pl.pallas_call(
        _moe_kernel,
        grid_spec=pltpu.PrefetchScalarGridSpec(
            num_scalar_prefetch=2,
            grid=(p // tm, nj),
            in_specs=[
                pl.BlockSpec((tm, d), rows),
                pl.BlockSpec((1, d, tf), lambda i, j, te, nu: (te[i], 0, jj(i, j, nu))),
                pl.BlockSpec((1, d, tf), lambda i, j, te, nu: (te[i], 0, jj(i, j, nu))),
                pl.BlockSpec((1, tf, d), lambda i, j, te, nu: (te[i], jj(i, j, nu), 0)),
            ],
            out_specs=pl.BlockSpec((tm, d), lambda i, j, te, nu: (i, 0)),
            scratch_shapes=[pltpu.VMEM((tm, d), F32)],
        ),
        out_shape=jax.ShapeDtypeStruct((p, d), BF16),
        compiler_params=_cparams("arbitrary", "arbitrary"),
        name="moe_ffn",
    )(tile_expert, n_used, xs, wg, wu, wd)


def _combine_ln_kernel(alpha, x_ref, y1_ref, y2_ref, w_ref, g_ref, b_ref, o_ref, obf_ref):
    w = w_ref[...]
    f = w[:, 0:1] * y1_ref[...].astype(F32) + w[:, 1:2] * y2_ref[...].astype(F32)
    y = _layer_norm(alpha * x_ref[...] + f, g_ref[...], b_ref[...])
    o_ref[...] = y
    obf_ref[...] = y.astype(BF16)


def _combine_ln(x, y1, y2, wcol, gain, bias, alpha, tm=512):
    t, d = x.shape
    row = lambda i: (i, 0)
    const = lambda i: (0, 0)
    return pl.pallas_call(
        functools.partial(_combine_ln_kernel, alpha),
        grid=(t // tm,),
        in_specs=[
            pl.BlockSpec((tm, d), row),
            pl.BlockSpec((tm, d), row),
            pl.BlockSpec((tm, d), row),
            pl.BlockSpec((tm, 2), row),
            pl.BlockSpec((1, d), const),
            pl.BlockSpec((1, d), const),
        ],
        out_specs=[pl.BlockSpec((tm, d), row), pl.BlockSpec((tm, d), row)],
        out_shape=[jax.ShapeDtypeStruct((t, d), F32), jax.ShapeDtypeStruct((t, d), BF16)],
        compiler_params=_cparams("parallel"),
        name="combine_ln",
    )(x, y1, y2, wcol, gain.reshape(1, d), bias.reshape(1, d))


def _moe(x, xbf, route, wg, wu, wd, gain, bias, alpha, tm=512):
    t, d = x.shape
    ne = wg.shape[0]
    ids = route[0:2].astype(jnp.int32).reshape(-1)
    tok = jnp.tile(jnp.arange(t, dtype=jnp.int32), 2)
    onehot = (ids[:, None] == jnp.arange(ne, dtype=jnp.int32)[None, :]).astype(jnp.int32)
    csum = jnp.cumsum(onehot, axis=0)
    rank = jnp.sum(onehot * (csum - 1), axis=1)
    counts = csum[-1]
    padded = ((counts + tm - 1) // tm) * tm
    ends = jnp.cumsum(padded)
    starts = ends - padded
    pos = jnp.sum(onehot * starts[None, :], axis=1) + rank
    n_rows = 2 * t + ne * tm
    n_tiles = n_rows // tm
    src = jnp.zeros((n_rows,), jnp.int32).at[pos].set(tok, unique_indices=True)
    tile_start = jnp.arange(n_tiles, dtype=jnp.int32) * tm
    tile_expert = jnp.minimum(jnp.sum((tile_start[:, None] >= ends[None, :]).astype(jnp.int32), axis=1), ne - 1)
    n_used = (ends[-1] // tm).astype(jnp.int32)
    last_expert = tile_expert[jnp.maximum(n_used - 1, 0)]
    tile_expert = jnp.where(jnp.arange(n_tiles) < n_used, tile_expert, last_expert).astype(jnp.int32)
    xs = jnp.take(xbf, src, axis=0)
    ys = _moe_ffn(xs, wg, wu, wd, tile_expert, n_used.reshape(1), tm)
    y1 = jnp.take(ys, pos[:t], axis=0)
    y2 = jnp.take(ys, pos[t:], axis=0)
    wcol = route[2:4].T
    return _combine_ln(x, y1, y2, wcol, gain, bias, alpha)


def _prep_a_w_in(w):
    d = w.shape[0]
    nq = d
    nkv = KV_HEADS * HEAD_DIM
    q = w[:, :nq][:, _pair_cols(nq // HEAD_DIM)]
    k = w[:, nq:nq + nkv][:, _kdup_cols(KV_HEADS)]
    v = w[:, nq + nkv:][:, _vdup_cols(KV_HEADS)]
    return jnp.concatenate([q, k, v], axis=1).astype(BF16)


def _prep_b_w_in(w):
    d = w.shape[0]
    nq = d
    nkv = KV_HEADS * HEAD_DIM
    part = lambda i: w[:, nq + i * nkv:nq + (i + 1) * nkv]
    q = w[:, :nq][:, _pair_cols(nq // HEAD_DIM)]
    kc = part(0)[:, _pair_cols(KV_HEADS)]
    vc = part(1)
    ks = part(2)[:, _kdup_cols(KV_HEADS)]
    vs = part(3)[:, _vdup_cols(KV_HEADS)]
    kw = part(4)[:, _kdup_cols(KV_HEADS)]
    vw = part(5)[:, _vdup_cols(KV_HEADS)]
    gl = w[:, nq + 6 * nkv:]
    per = GROUP * N_GATES
    gcols = []
    for g in range(KV_HEADS):
        gcols.append(jnp.pad(gl[:, g * per:(g + 1) * per], ((0, 0), (0, LANES - per))))
    return jnp.concatenate([q, kc, ks, kw, vc, vs, vw] + gcols, axis=1).astype(BF16)


def _prep_compress(pos, w1, w2, rope_layout):
    hidden = w1.shape[1]
    w1 = w1.reshape(CMP_LEN, HEAD_DIM, hidden)
    zeros = jnp.zeros_like(w1)
    a = jnp.concatenate([w1, zeros], axis=2)
    b = jnp.concatenate([zeros, w1], axis=2)
    natural = jnp.concatenate([a, b], axis=1)
    pos2 = jnp.concatenate([pos, pos], axis=1)
    if rope_layout:
        cols = _pair_cols(2)
        natural, pos2 = natural[:, cols, :], pos2[:, cols]
        out_cols = _kdup_cols(1)
    else:
        out_cols = _vdup_cols(1)
    w2d = w2[:, out_cols]
    z2 = jnp.zeros_like(w2d)
    w2p = jnp.concatenate([jnp.concatenate([w2d, z2], axis=1), jnp.concatenate([z2, w2d], axis=1)], axis=0)
    return pos2.astype(F32), natural.astype(BF16), w2p.astype(BF16)


def _overlap_t(ncp, nsel):
    cs = CMP_STRIDE * np.arange(ncp)
    ce = cs + CMP_LEN
    ss = SEL_BLOCK * np.arange(nsel)
    se = ss + SEL_BLOCK
    ov = np.clip(np.minimum(ce[None, :], se[:, None]) - np.maximum(cs[None, :], ss[:, None]), 0, None)
    return jnp.asarray(ov / CMP_STRIDE, dtype=BF16)


def _key_block_indicator(seq):
    assert seq // SEL_BLOCK <= LANES
    hit = (np.arange(seq) // SEL_BLOCK)[:, None] == np.arange(LANES)[None, :]
    return jnp.asarray(np.where(hit, SEL_MASK, 0.0), dtype=BF16)


def kernel(x, a_w_in, a_w_out, a_sinks, b_w_in, b_w_out, b_cmp_pos_k, b_cmp_pos_v, b_cmp_k_w1, b_cmp_k_w2, b_cmp_v_w1, b_cmp_v_w2, ffn_w_gate, ffn_w_up, ffn_w_down, moe_router, moe_w_gate, moe_w_up, moe_w_down, ln_gain, ln_bias):
    batch, seq, d = x.shape
    depth = ln_gain.shape[0]
    alpha = float((2 * depth) ** 0.25)
    tables = _rope_tables(seq)
    xf = x.reshape(batch * seq, d)
    xbf = xf
    for i in range(depth):
        j = i // 2
        if i % 2 == 0:
            q, k2, v2 = _proj_a(xbf, _prep_a_w_in(a_w_in[j]), tables, seq)
            o = _swa_attention(q, k2, v2, a_sinks[j], batch, seq)
            xf, xbf = _outproj_ln(o, a_w_out[j].astype(BF16), xf, ln_gain[i, 0], ln_bias[i, 0], alpha)
            xf, xbf = _ffn_ln(xbf, xf, ffn_w_gate[j].astype(BF16), ffn_w_up[j].astype(BF16),
                              ffn_w_down[j].astype(BF16), ln_gain[i, 1], ln_bias[i, 1], alpha)
        else:
            q, kc, ks2, kw2, vc, vs2, vw2, gates = _proj_b(xbf, _prep_b_w_in(b_w_in[j]), tables, seq)
            pk, w1k, w2k = _prep_compress(b_cmp_pos_k[j], b_cmp_k_w1[j], b_cmp_k_w2[j], True)
            pv, w1v, w2v = _prep_compress(b_cmp_pos_v[j], b_cmp_v_w1[j], b_cmp_v_w2[j], False)
            kcc, vcc = _compress(kc, vc, pk, pv, w1k, w1v, w2k, w2v, batch, seq)
            ovt = _overlap_t(seq // CMP_STRIDE, seq // SEL_BLOCK)
            o = _nsa_attention(q, gates, kcc, vcc, ks2, vs2, kw2, vw2, ovt,
                               _key_block_indicator(seq), batch, seq)
            xf, xbf, route = _outproj_ln(o, b_w_out[j].astype(BF16), xf, ln_gain[i, 0], ln_bias[i, 0], alpha,
                                         router_t=moe_router[j].T)
            xf, xbf = _moe(xf, xbf, route, moe_w_gate[j].astype(BF16), moe_w_up[j].astype(BF16),
                           moe_w_down[j].astype(BF16), ln_gain[i, 1], ln_bias[i, 1], alpha)
    return xf.reshape(batch, seq, d)
```

```python
import functools
import math

import numpy as np
import jax
import jax.numpy as jnp
from jax import lax
from jax.experimental import pallas as pl
from jax.experimental.pallas import tpu as pltpu

F32 = jnp.float32
BF16 = jnp.bfloat16

HEAD_DIM = 64
HALF_DIM = HEAD_DIM // 2
LANES = 128
MXU_WIDTH = 256
ROPE_THETA = 10000.0
LOG2E = math.log2(math.e)
Q_SCALE = HEAD_DIM ** -0.5 * LOG2E
KV_HEADS = 4
GROUP = 4
SWA_BLOCK = 128
SWA_WINDOW = 128
CMP_LEN = 32
CMP_STRIDE = 16
SEL_BLOCK = 64
SEL_SHIFT = 6
SEL_TOPN = 16
NSA_WINDOW = 512
SEL_FORCE = 1e4
N_GATES = 3
LN_EPS = 1e-5
NEG_INF = -1e30
SEL_MASK = 2.0 ** 100
VMEM_LIMIT = 48 * 1024 * 1024

NSA_TQ = 128
NSA_KC = 512


def _cparams(*sem):
    return pltpu.CompilerParams(dimension_semantics=sem, vmem_limit_bytes=VMEM_LIMIT)


def _dot(a, b):
    return jnp.dot(a, b, preferred_element_type=F32)


def _dot_nt(a, b):
    return lax.dot_general(a, b, (((1,), (1,)), ((), ())), preferred_element_type=F32)


def _dot_tn(a, b):
    return lax.dot_general(a, b, (((0,), (0,)), ((), ())), preferred_element_type=F32)


def _pair_cols(n_heads):
    idx = []
    for j in range(n_heads // 2):
        a, b = 2 * j * HEAD_DIM, (2 * j + 1) * HEAD_DIM
        idx += list(range(a, a + HALF_DIM)) + list(range(b, b + HALF_DIM))
        idx += list(range(a + HALF_DIM, a + HEAD_DIM)) + list(range(b + HALF_DIM, b + HEAD_DIM))
    return np.asarray(idx, np.int32)


def _kdup_cols(n_heads):
    idx = []
    for h in range(n_heads):
        a = h * HEAD_DIM
        idx += list(range(a, a + HALF_DIM)) * 2 + list(range(a + HALF_DIM, a + HEAD_DIM)) * 2
    return np.asarray(idx, np.int32)


def _vdup_cols(n_heads):
    idx = []
    for h in range(n_heads):
        idx += list(range(h * HEAD_DIM, (h + 1) * HEAD_DIM)) * 2
    return np.asarray(idx, np.int32)


def _rope_tables(seq):
    inv = 1.0 / (ROPE_THETA ** (jnp.arange(0, HEAD_DIM, 2, dtype=F32) / HEAD_DIM))
    ang = jnp.arange(seq, dtype=F32)[:, None] * inv[None, :]
    cos, sin = jnp.cos(ang), jnp.sin(ang)
    c, s = jnp.tile(cos, (1, 4)), jnp.concatenate([-sin, -sin, sin, sin], axis=1)
    return jnp.concatenate([c * Q_SCALE, s * Q_SCALE, c, s], axis=1)


def _rope(y, c, s):
    return y * c + pltpu.roll(y, LANES // 2, 1) * s


def _project_rope(x, w_ref, col, out_ref, c, s):
    width = out_ref.shape[1]
    step = min(MXU_WIDTH, width)
    for j in range(width // step):
        y = _dot(x, w_ref[:, col + j * step:col + (j + 1) * step])
        for i in range(step // LANES):
            lo = j * step + i * LANES
            out_ref[:, lo:lo + LANES] = _rope(y[:, i * LANES:(i + 1) * LANES], c, s).astype(out_ref.dtype)


def _split_tables(tab_ref):
    tab = tab_ref[...]
    return [tab[:, i * LANES:(i + 1) * LANES] for i in range(4)]


def _layer_norm(z, g, b):
    mu = jnp.mean(z, axis=-1, keepdims=True)
    zc = z - mu
    var = jnp.mean(zc * zc, axis=-1, keepdims=True)
    return zc * lax.rsqrt(var + LN_EPS) * g + b


def _proj_a_kernel(x_ref, w_ref, tab_ref, q_ref, k_ref, v_ref):
    x = x_ref[...].astype(BF16)
    cq, sq, c, s = _split_tables(tab_ref)
    nq = q_ref.shape[1]
    nk = k_ref.shape[1]
    _project_rope(x, w_ref, 0, q_ref, cq, sq)
    _project_rope(x, w_ref, nq, k_ref, c, s)
    v_ref[...] = _dot(x, w_ref[:, nq + nk:]).astype(BF16)


def _proj_a(x2d, w, tables, seq, tm=512):
    t, d = x2d.shape
    nq = d
    nk = KV_HEADS * LANES
    per_seq = seq // tm
    return pl.pallas_call(
        _proj_a_kernel,
        grid=(t // tm,),
        in_specs=[
            pl.BlockSpec((tm, d), lambda i: (i, 0)),
            pl.BlockSpec(w.shape, lambda i: (0, 0)),
            pl.BlockSpec((tm, 4 * LANES), lambda i: (i % per_seq, 0)),
        ],
        out_specs=[
            pl.BlockSpec((tm, nq), lambda i: (i, 0)),
            pl.BlockSpec((tm, nk), lambda i: (i, 0)),
            pl.BlockSpec((tm, nk), lambda i: (i, 0)),
        ],
        out_shape=[
            jax.ShapeDtypeStruct((t, nq), BF16),
            jax.ShapeDtypeStruct((t, nk), BF16),
            jax.ShapeDtypeStruct((t, nk), BF16),
        ],
        compiler_params=_cparams("parallel"),
        name="proj_a",
    )(x2d, w, tables)


def _proj_b_kernel(x_ref, w_ref, tab_ref,
                   q_ref, kc_ref, ks_ref, kw_ref, vc_ref, vs_ref, vw_ref, g_ref):
    x = x_ref[...].astype(BF16)
    cq, sq, c, s = _split_tables(tab_ref)
    col = 0
    for ref in (q_ref, kc_ref, ks_ref, kw_ref):
        cr, sr = (cq, sq) if ref is q_ref else (c, s)
        _project_rope(x, w_ref, col, ref, cr, sr)
        col += ref.shape[1]
    for ref in (vc_ref, vs_ref, vw_ref):
        n = ref.shape[1]
        ref[...] = _dot(x, w_ref[:, col:col + n]).astype(ref.dtype)
        col += n
    g_ref[...] = jax.nn.sigmoid(_dot(x, w_ref[:, col:]))


def _proj_b(x2d, w, tables, seq, tm=512):
    t, d = x2d.shape
    nkv = KV_HEADS * HEAD_DIM
    ndup = KV_HEADS * LANES
    widths = [(d, BF16), (nkv, F32), (ndup, BF16), (ndup, BF16),
              (nkv, F32), (ndup, BF16), (ndup, BF16), (KV_HEADS * LANES, F32)]
    per_seq = seq // tm
    return pl.pallas_call(
        _proj_b_kernel,
        grid=(t // tm,),
        in_specs=[
            pl.BlockSpec((tm, d), lambda i: (i, 0)),
            pl.BlockSpec(w.shape, lambda i: (0, 0)),
            pl.BlockSpec((tm, 4 * LANES), lambda i: (i % per_seq, 0)),
        ],
        out_specs=[pl.BlockSpec((tm, n), lambda i: (i, 0)) for n, _ in widths],
        out_shape=[jax.ShapeDtypeStruct((t, n), dt) for n, dt in widths],
        compiler_params=_cparams("parallel"),
        name="proj_b",
    )(x2d, w, tables)


def _stack_heads(q_ref, rows):
    lane = lax.broadcasted_iota(jnp.int32, (rows, LANES), 1)
    first = (lane & (HEAD_DIM - 1)) < HALF_DIM
    zero = jnp.zeros((rows, LANES), BF16)
    parts = []
    for p in range(GROUP // 2):
        qp = q_ref[:, p * LANES:(p + 1) * LANES]
        parts += [jnp.where(first, qp, zero), jnp.where(first, zero, qp)]
    return jnp.concatenate(parts, axis=0)


def _unstack_heads(o, rows):
    lane = lax.broadcasted_iota(jnp.int32, (rows, LANES), 1)
    low = lane < HEAD_DIM
    return [jnp.where(low, o[2 * p], o[2 * p + 1]) for p in range(GROUP // 2)]


def _swa_kernel(sink_ref, q_ref, kp_ref, kc_ref, vp_ref, vc_ref, o_ref):
    n = pl.program_id(1)
    blk = SWA_BLOCK
    qi = lax.broadcasted_iota(jnp.int32, (blk, 2 * blk), 0)
    si = lax.broadcasted_iota(jnp.int32, (blk, 2 * blk), 1)
    diff = qi + blk - si
    ok = (diff >= 0) & (diff < SWA_WINDOW) & ((si >= blk) | (n > 0))
    for g in range(KV_HEADS):
        q4 = _stack_heads(q_ref.at[:, g * 2 * LANES:(g + 1) * 2 * LANES], blk)
        kk = jnp.concatenate([kp_ref[:, g * LANES:(g + 1) * LANES], kc_ref[:, g * LANES:(g + 1) * LANES]], axis=0)
        vv = jnp.concatenate([vp_ref[:, g * LANES:(g + 1) * LANES], vc_ref[:, g * LANES:(g + 1) * LANES]], axis=0)
        s = _dot_nt(q4, kk).reshape(GROUP, blk, 2 * blk)
        s = jnp.where(ok[None], s, NEG_INF)
        ps, rs = [], []
        for h in range(GROUP):
            sink = sink_ref[g * GROUP + h] * LOG2E
            m = jnp.maximum(jnp.max(s[h], axis=-1, keepdims=True), sink)
            p = jnp.exp2(s[h] - m)
            rs.append(1.0 / (jnp.sum(p, axis=-1, keepdims=True) + jnp.exp2(sink - m)))
            ps.append(p.astype(BF16))
        o = _dot(jnp.concatenate(ps, axis=0), vv).reshape(GROUP, blk, LANES)
        o = [o[h] * rs[h] for h in range(GROUP)]
        for p, blkout in enumerate(_unstack_heads(o, blk)):
            c0 = (g * 2 + p) * LANES
            o_ref[:, c0:c0 + LANES] = blkout.astype(BF16)


def _swa_attention(q, k2, v2, sinks, batch, seq):
    t, nq = q.shape
    nk = k2.shape[1]
    nb = seq // SWA_BLOCK
    cur = lambda b, n: (b * nb + n, 0)
    prev = lambda b, n: (b * nb + jnp.maximum(n - 1, 0), 0)
    return pl.pallas_call(
        _swa_kernel,
        grid=(batch, nb),
        in_specs=[
            pl.BlockSpec(memory_space=pltpu.SMEM),
            pl.BlockSpec((SWA_BLOCK, nq), cur),
            pl.BlockSpec((SWA_BLOCK, nk), prev),
            pl.BlockSpec((SWA_BLOCK, nk), cur),
            pl.BlockSpec((SWA_BLOCK, nk), prev),
            pl.BlockSpec((SWA_BLOCK, nk), cur),
        ],
        out_specs=pl.BlockSpec((SWA_BLOCK, nq), cur),
        out_shape=jax.ShapeDtypeStruct((t, nq), BF16),
        compiler_params=_cparams("parallel", "parallel"),
        name="swa_attention",
    )(sinks, q, k2, k2, v2, v2)


def _outproj_ln_kernel(alpha, o_ref, w_ref, x_ref, g_ref, b_ref, y_ref, ybf_ref):
    z = alpha * x_ref[...].astype(F32) + _dot(o_ref[...], w_ref[...])
    y = _layer_norm(z, g_ref[...], b_ref[...])
    y_ref[...] = y
    ybf_ref[...] = y.astype(BF16)


def _outproj_ln_router_kernel(alpha, o_ref, w_ref, x_ref, g_ref, b_ref, r_ref, y_ref, ybf_ref, route_ref):
    z = alpha * x_ref[...].astype(F32) + _dot(o_ref[...], w_ref[...])
    y = _layer_norm(z, g_ref[...], b_ref[...])
    y_ref[...] = y
    ybf_ref[...] = y.astype(BF16)
    ne = r_ref.shape[0] // 2
    y_hi = y.astype(BF16)
    y_lo = (y - y_hi.astype(F32)).astype(BF16)
    part = _dot_nt(r_ref[...], y_hi)
    logits = part[0:ne] + part[ne:2 * ne] + _dot_nt(r_ref[0:ne, :], y_lo)
    eid = lax.broadcasted_iota(jnp.int32, logits.shape, 0)
    m1 = jnp.max(logits, axis=0, keepdims=True)
    i1 = jnp.min(jnp.where(logits == m1, eid, ne), axis=0, keepdims=True)
    rest = jnp.where(eid == i1, -jnp.inf, logits)
    m2 = jnp.max(rest, axis=0, keepdims=True)
    i2 = jnp.min(jnp.where(rest == m2, eid, ne), axis=0, keepdims=True)
    e = jnp.exp(m2 - m1)
    w1 = 1.0 / (1.0 + e)
    w2 = e / (1.0 + e)
    route_ref[...] = jnp.concatenate(
        [i1.astype(F32), i2.astype(F32), w1, w2, jnp.zeros((4, logits.shape[1]), F32)], axis=0)


def _outproj_ln(o, w, x, gain, bias, alpha, router_t=None, tm=512):
    t, d = x.shape
    row = lambda i: (i, 0)
    const = lambda i: (0, 0)
    in_specs = [
        pl.BlockSpec((tm, o.shape[1]), row),
        pl.BlockSpec(w.shape, const),
        pl.BlockSpec((tm, d), row),
        pl.BlockSpec((1, d), const),
        pl.BlockSpec((1, d), const),
    ]
    out_specs = [pl.BlockSpec((tm, d), row), pl.BlockSpec((tm, d), row)]
    out_shape = [jax.ShapeDtypeStruct((t, d), F32), jax.ShapeDtypeStruct((t, d), BF16)]
    args = [o, w, x, gain.reshape(1, d), bias.reshape(1, d)]
    if router_t is None:
        body = functools.partial(_outproj_ln_kernel, alpha)
        name = "outproj_ln"
    else:
        body = functools.partial(_outproj_ln_router_kernel, alpha)
        name = "outproj_ln_router"
        in_specs.append(pl.BlockSpec(router_t.shape, const))
        args.append(router_t)
        out_specs.append(pl.BlockSpec((8, tm), lambda i: (0, i)))
        out_shape.append(jax.ShapeDtypeStruct((8, t), F32))
    return pl.pallas_call(
        body,
        grid=(t // tm,),
        in_specs=in_specs,
        out_specs=out_specs,
        out_shape=out_shape,
        compiler_params=_cparams("parallel"),
        name=name,
    )(*args)


def _ffn_ln_kernel(alpha, xbf_ref, x_ref, wg_ref, wu_ref, wd_ref, g_ref, b_ref, y_ref, ybf_ref, acc_ref):
    j = pl.program_id(1)

    @pl.when(j == 0)
    def _():
        acc_ref[...] = jnp.zeros_like(acc_ref)

    xb = xbf_ref[...]
    h = jax.nn.silu(_dot(xb, wg_ref[...])) * _dot(xb, wu_ref[...])
    acc_ref[...] += _dot(h.astype(BF16), wd_ref[...])

    @pl.when(j == pl.num_programs(1) - 1)
    def _():
        y = _layer_norm(alpha * x_ref[...] + acc_ref[...], g_ref[...], b_ref[...])
        y_ref[...] = y
        ybf_ref[...] = y.astype(BF16)


def _ffn_ln(xbf, x, wg, wu, wd, gain, bias, alpha, tm=1024, tf=512):
    t, d = x.shape
    dff = wg.shape[1]
    row = lambda i, j: (i, 0)
    const = lambda i, j: (0, 0)
    return pl.pallas_call(
        functools.partial(_ffn_ln_kernel, alpha),
        grid=(t // tm, dff // tf),
        in_specs=[
            pl.BlockSpec((tm, d), row),
            pl.BlockSpec((tm, d), row),
            pl.BlockSpec((d, tf), lambda i, j: (0, j)),
            pl.BlockSpec((d, tf), lambda i, j: (0, j)),
            pl.BlockSpec((tf, d), lambda i, j: (j, 0)),
            pl.BlockSpec((1, d), const),
            pl.BlockSpec((1, d), const),
        ],
        out_specs=[pl.BlockSpec((tm, d), row), pl.BlockSpec((tm, d), row)],
        out_shape=[jax.ShapeDtypeStruct((t, d), F32), jax.ShapeDtypeStruct((t, d), BF16)],
        scratch_shapes=[pltpu.VMEM((tm, d), F32)],
        compiler_params=_cparams("parallel", "arbitrary"),
        name="ffn_ln",
    )(xbf, x, wg, wu, wd, gain.reshape(1, d), bias.reshape(1, d))


def _gelu_tanh(x):
    return 0.5 * x * (1.0 + jnp.tanh(math.sqrt(2.0 / math.pi) * (x + 0.044715 * (x * x * x))))


def _compress_kernel(kc_ref, vc_ref, pk_ref, pv_ref, w1k_ref, w1v_ref, w2k_ref, w2v_ref, ko_ref, vo_ref):
    nseg = ko_ref.shape[0]
    half = CMP_LEN // 2

    def one(t_ref, pos_ref, w1_ref, w2_ref, out_ref):
        hidden = w1_ref.shape[2]
        lo = jnp.zeros((nseg, hidden), F32)
        hi = jnp.zeros((nseg, hidden), F32)
        for l in range(half):
            rows = t_ref[pl.ds(l, nseg, stride=CMP_STRIDE), :]
            lo += _dot((rows + pos_ref[l:l + 1, :]).astype(BF16), w1_ref[l])
            hi += _dot((rows + pos_ref[half + l:half + l + 1, :]).astype(BF16), w1_ref[half + l])
        h = _gelu_tanh(lo + pltpu.roll(hi, nseg - 1, 0))
        out_ref[...] = _dot(h.astype(BF16), w2_ref[...]).astype(out_ref.dtype)

    one(kc_ref, pk_ref, w1k_ref, w2k_ref, ko_ref)
    one(vc_ref, pv_ref, w1v_ref, w2v_ref, vo_ref)


def _compress(kc, vc, pk, pv, w1k, w1v, w2k, w2v, batch, seq):
    nseg = seq // CMP_STRIDE
    npair = KV_HEADS // 2
    tok = lambda b, p: (b, p)
    c2 = lambda b, p: (0, 0)
    c3 = lambda b, p: (0, 0, 0)
    return pl.pallas_call(
        _compress_kernel,
        grid=(batch, npair),
        in_specs=[
            pl.BlockSpec((seq, LANES), tok),
            pl.BlockSpec((seq, LANES), tok),
            pl.BlockSpec(pk.shape, c2),
            pl.BlockSpec(pv.shape, c2),
            pl.BlockSpec(w1k.shape, c3),
            pl.BlockSpec(w1v.shape, c3),
            pl.BlockSpec(w2k.shape, c2),
            pl.BlockSpec(w2v.shape, c2),
        ],
        out_specs=[pl.BlockSpec((nseg, 2 * LANES), tok), pl.BlockSpec((nseg, 2 * LANES), tok)],
        out_shape=[jax.ShapeDtypeStruct((batch * nseg, KV_HEADS * LANES), BF16)] * 2,
        compiler_params=_cparams("parallel", "parallel"),
        name="nsa_compress",
    )(kc, vc, pk, pv, w1k, w1v, w2k, w2v)


def _lane_fold(x, op):
    out = x[:, 0:LANES]
    for j in range(1, x.shape[1] // LANES):
        out = op(out, x[:, j * LANES:(j + 1) * LANES])
    return out


def _nsa_kernel(q_ref, gate_ref, kcc_ref, vcc_ref, ks_ref, vs_ref, kw_ref, vw_ref, ovt_ref, blk_ref,
                o_ref, qa_ref, imp_ref, s_ref, mx_ref, l_ref, acc_ref, part_ref):
    tq, kc = NSA_TQ, NSA_KC
    rows = GROUP * tq
    ncp = kcc_ref.shape[0]
    nsel = ovt_ref.shape[0]
    topn = min(SEL_TOPN, nsel)
    t0 = pl.program_id(2) * tq
    qa_ref[:, 0:LANES] = _stack_heads(q_ref, tq)
    tpos = t0 + lax.broadcasted_iota(jnp.int32, (tq, 1), 0)
    gates = gate_ref[...]

    def gate_column(branch):
        return jnp.concatenate([gates[:, N_GATES * h + branch:N_GATES * h + branch + 1] for h in range(GROUP)],
                               axis=0)

    cend = CMP_STRIDE * lax.broadcasted_iota(jnp.int32, (1, ncp), 1) + (CMP_LEN - 1)
    ok_c = (cend <= tpos)[None]
    s = jnp.where(ok_c, _dot_nt(qa_ref[:, 0:LANES], kcc_ref[...]).reshape(GROUP, tq, ncp), NEG_INF)
    p = jnp.where(ok_c, jnp.exp2(s - jnp.max(s, axis=-1, keepdims=True)), 0.0)
    p = p * (1.0 / jnp.maximum(jnp.sum(p, axis=-1, keepdims=True), 1e-30))
    o_c = _dot(p.reshape(rows, ncp).astype(BF16), vcc_ref[...])
    psum = p[0] + p[1] + p[2] + p[3]

    span = NSA_WINDOW + tq
    w0 = pl.multiple_of(jnp.maximum(t0 - NSA_WINDOW, 0), LANES)
    d = tpos - (w0 + lax.broadcasted_iota(jnp.int32, (1, span), 1))
    bias = jnp.where((d >= 0) & (d < NSA_WINDOW), 0.0, NEG_INF)
    s = _dot_nt(qa_ref[:, 0:LANES], kw_ref[pl.ds(w0, span), :])
    s = (s.reshape(GROUP, tq, span) + bias[None]).reshape(rows, span)
    mw = jnp.broadcast_to(jnp.max(_lane_fold(s, jnp.maximum), axis=-1, keepdims=True), (rows, LANES))
    p = [jnp.exp2(s[:, j * LANES:(j + 1) * LANES] - mw) for j in range(span // LANES)]
    lw = jnp.sum(functools.reduce(jnp.add, p), axis=-1, keepdims=True)
    o_w = _dot(jnp.concatenate([x.astype(BF16) for x in p], axis=1), vw_ref[pl.ds(w0, span), :])
    part_ref[...] = gate_column(0) * o_c + (gate_column(2) / lw) * o_w

    ovt = ovt_ref[...]
    imp = jnp.zeros((nsel, tq), F32)
    rem = psum
    for _ in range(3):
        part = rem.astype(BF16)
        imp += _dot_nt(ovt, part)
        rem = rem - part.astype(F32)
    blk = lax.broadcasted_iota(jnp.int32, (nsel, tq), 0)
    cur = jnp.right_shift(t0 + lax.broadcasted_iota(jnp.int32, (nsel, tq), 1), SEL_SHIFT)
    forced = (blk == 0) | (blk == cur) | (blk == cur - 1)
    imp = jnp.where(forced, SEL_FORCE, imp)
    imp = jnp.where(blk <= cur, imp, -1.0)
    imp_ref[...] = imp

    ngrp = nsel // 8
    sub = lax.broadcasted_iota(jnp.int32, (8, tq), 0)
    grp = [imp[8 * r:8 * r + 8] for r in range(ngrp)]
    cnt = [jnp.zeros((8, tq), F32) for _ in range(ngrp)]
    for i in range(nsel):
        row = jnp.broadcast_to(imp_ref[i:i + 1, :], (8, tq))
        for r in range(ngrp):
            ge = jnp.where(row >= grp[r], 1.0, 0.0)
            gt = jnp.where(row > grp[r], 1.0, 0.0)
            if 8 * r > i:
                cnt[r] = cnt[r] + ge
            elif 8 * r + 7 < i:
                cnt[r] = cnt[r] + gt
            else:
                cnt[r] = cnt[r] + jnp.where(sub > i - 8 * r, ge, gt)
    rank = jnp.concatenate(cnt, axis=0)
    unsel = jnp.where((rank < topn) & (blk <= cur), 0.0, -1.0)
    if nsel < LANES:
        unsel = jnp.concatenate([unsel, jnp.zeros((LANES - nsel, tq), F32)], axis=0)
    selq = unsel.T.astype(BF16)
    for h in range(GROUP):
        qa_ref[h * tq:(h + 1) * tq, LANES:2 * LANES] = selq

    def key_side(start):
        return jnp.concatenate([ks_ref[pl.ds(start, kc), :], blk_ref[pl.ds(start, kc), :]], axis=1)

    nfull = t0 // kc
    dstart = pl.multiple_of(nfull * kc, kc)
    causal = jnp.where(dstart + lax.broadcasted_iota(jnp.int32, (1, kc), 1) <= tpos, 0.0, NEG_INF)
    s = (_dot_nt(qa_ref[...], key_side(dstart)).reshape(GROUP, tq, kc) + causal[None]).reshape(rows, kc)
    s_ref[nfull] = s
    mx_ref[...] = _lane_fold(s, jnp.maximum)

    def score_chunks(first, count):
        mx = mx_ref[...]
        for u in range(count):
            s = _dot_nt(qa_ref[...], key_side(pl.multiple_of((first + u) * kc, kc)))
            s_ref[first + u] = s
            mx = jnp.maximum(mx, _lane_fold(s, jnp.maximum))
        mx_ref[...] = mx

    def score_pair(c, carry):
        score_chunks(2 * c, 2)
        return carry

    lax.fori_loop(0, nfull // 2, score_pair, 0)

    @pl.when(nfull % 2 == 1)
    def _():
        score_chunks(nfull - 1, 1)

    mx_ref[...] = jnp.broadcast_to(jnp.max(mx_ref[...], axis=-1, keepdims=True), (rows, LANES))
    l_ref[...] = jnp.zeros(l_ref.shape, F32)
    acc_ref[...] = jnp.zeros(acc_ref.shape, F32)

    def value_chunks(first, count):
        mrep = mx_ref[...]
        lsum = l_ref[...]
        acc = acc_ref[...]
        for u in range(count):
            sc = s_ref[first + u]
            p = [jnp.exp2(sc[:, j * LANES:(j + 1) * LANES] - mrep) for j in range(kc // LANES)]
            lsum = lsum + functools.reduce(jnp.add, p)
            pb = jnp.concatenate([x.astype(BF16) for x in p], axis=1)
            acc = acc + _dot(pb, vs_ref[pl.ds(pl.multiple_of((first + u) * kc, kc), kc), :])
        l_ref[...] = lsum
        acc_ref[...] = acc

    def value_pair(c, carry):
        value_chunks(2 * c, 2)
        return carry

    lax.fori_loop(0, (nfull + 1) // 2, value_pair, 0)

    @pl.when(nfull % 2 == 0)
    def _():
        value_chunks(nfull, 1)

    ls = jnp.sum(l_ref[...], axis=-1, keepdims=True)
    out = part_ref[...] + (gate_column(1) / ls) * acc_ref[...]
    for pidx, blkout in enumerate(_unstack_heads(out.reshape(GROUP, tq, LANES), tq)):
        o_ref[:, pidx * LANES:(pidx + 1) * LANES] = blkout.astype(BF16)


def _nsa_attention(q, gates, kcc, vcc, ks2, vs2, kw2, vw2, ovt, key_blk, batch, seq):
    t, nq = q.shape
    tq = NSA_TQ
    rows = GROUP * tq
    nt = seq // tq
    ncp = seq // CMP_STRIDE
    nsel = seq // SEL_BLOCK
    qmap = lambda b, g, i: (b * nt + i, g)
    kvmap = lambda b, g, i: (b, g)
    return pl.pallas_call(
        _nsa_kernel,
        grid=(batch, KV_HEADS, nt),
        in_specs=[
            pl.BlockSpec((tq, 2 * LANES), qmap),
            pl.BlockSpec((tq, LANES), qmap),
            pl.BlockSpec((ncp, LANES), kvmap),
            pl.BlockSpec((ncp, LANES), kvmap),
            pl.BlockSpec((seq, LANES), kvmap),
            pl.BlockSpec((seq, LANES), kvmap),
            pl.BlockSpec((seq, LANES), kvmap),
            pl.BlockSpec((seq, LANES), kvmap),
            pl.BlockSpec(ovt.shape, lambda b, g, i: (0, 0)),
            pl.BlockSpec(key_blk.shape, lambda b, g, i: (0, 0)),
        ],
        out_specs=pl.BlockSpec((tq, 2 * LANES), qmap),
        out_shape=jax.ShapeDtypeStruct((t, nq), BF16),
        scratch_shapes=[
            pltpu.VMEM((rows, 2 * LANES), BF16),
            pltpu.VMEM((nsel, tq), F32),
            pltpu.VMEM((seq // NSA_KC, rows, NSA_KC), F32),
            pltpu.VMEM((rows, LANES), F32),
            pltpu.VMEM((rows, LANES), F32),
            pltpu.VMEM((rows, LANES), F32),
            pltpu.VMEM((rows, LANES), F32),
        ],
        compiler_params=_cparams("parallel", "parallel", "arbitrary"),
        name="nsa_attention",
    )(q, gates, kcc, vcc, ks2, vs2, kw2, vw2, ovt, key_blk)


def _moe_kernel(te_ref, nu_ref, xs_ref, wg_ref, wu_ref, wd_ref, ys_ref, acc_ref):
    i = pl.program_id(0)
    j = pl.program_id(1)

    @pl.when(i < nu_ref[0])
    def _():
        @pl.when(j == 0)
        def _():
            acc_ref[...] = jnp.zeros_like(acc_ref)

        xb = xs_ref[...]
        h = jax.nn.silu(_dot(xb, wg_ref[0])) * _dot(xb, wu_ref[0])
        acc_ref[...] += _dot(h.astype(BF16), wd_ref[0])

        @pl.when(j == pl.num_programs(1) - 1)
        def _():
            ys_ref[...] = acc_ref[...].astype(ys_ref.dtype)

    @pl.when((i >= nu_ref[0]) & (j == pl.num_programs(1) - 1))
    def _():
        ys_ref[...] = jnp.zeros_like(ys_ref)


def _moe_ffn(xs, wg, wu, wd, tile_expert, n_used, tm, tf=512):
    p, d = xs.shape
    dff = wg.shape[2]
    nj = dff // tf

    def rows(i, j, te, nu):
        return (jnp.minimum(i, nu[0] - 1), 0)

    def jj(i, j, nu):
        return jnp.where(i < nu[0], j, nj - 1)

    return pl.pallas_call(
        _moe_kernel,
        grid_spec=pltpu.PrefetchScalarGridSpec(
            num_scalar_prefetch=2,
            grid=(p // tm, nj),
            in_specs=[
                pl.BlockSpec((tm, d), rows),
                pl.BlockSpec((1, d, tf), lambda i, j, te, nu: (te[i], 0, jj(i, j, nu))),
                pl.BlockSpec((1, d, tf), lambda i, j, te, nu: (te[i], 0, jj(i, j, nu))),
                pl.BlockSpec((1, tf, d), lambda i, j, te, nu: (te[i], jj(i, j, nu), 0)),
            ],
            out_specs=pl.BlockSpec((tm, d), lambda i, j, te, nu: (i, 0)),
            scratch_shapes=[pltpu.VMEM((tm, d), F32)],
        ),
        out_shape=jax.ShapeDtypeStruct((p, d), BF16),
        compiler_params=_cparams("arbitrary", "arbitrary"),
        name="moe_ffn",
    )(tile_expert, n_used, xs, wg, wu, wd)


def _combine_ln_kernel(alpha, x_ref, y1_ref, y2_ref, w_ref, g_ref, b_ref, o_ref, obf_ref):
    w = w_ref[...]
    f = w[:, 0:1] * y1_ref[...].astype(F32) + w[:, 1:2] * y2_ref[...].astype(F32)
    y = _layer_norm(alpha * x_ref[...] + f, g_ref[...], b_ref[...])
    o_ref[...] = y
    obf_ref[...] = y.astype(BF16)


def _combine_ln(x, y1, y2, wcol, gain, bias, alpha, tm=512):
    t, d = x.shape
    row = lambda i: (i, 0)
    const = lambda i: (0, 0)
    return pl.pallas_call(
        functools.partial(_combine_ln_kernel, alpha),
        grid=(t // tm,),
        in_specs=[
            pl.BlockSpec((tm, d), row),
            pl.BlockSpec((tm, d), row),
            pl.BlockSpec((tm, d), row),
            pl.BlockSpec((tm, 2), row),
            pl.BlockSpec((1, d), const),
            pl.BlockSpec((1, d), const),
        ],
        out_specs=[pl.BlockSpec((tm, d), row), pl.BlockSpec((tm, d), row)],
        out_shape=[jax.ShapeDtypeStruct((t, d), F32), jax.ShapeDtypeStruct((t, d), BF16)],
        compiler_params=_cparams("parallel"),
        name="combine_ln",
    )(x, y1, y2, wcol, gain.reshape(1, d), bias.reshape(1, d))


def _moe(x, xbf, route, wg, wu, wd, gain, bias, alpha, tm=1024):
    t, d = x.shape
    ne = wg.shape[0]
    ids = route[0:2].astype(jnp.int32).reshape(-1)
    tok = jnp.tile(jnp.arange(t, dtype=jnp.int32), 2)
    onehot = (ids[:, None] == jnp.arange(ne, dtype=jnp.int32)[None, :]).astype(jnp.int32)
    csum = jnp.cumsum(onehot, axis=0)
    rank = jnp.sum(onehot * (csum - 1), axis=1)
    counts = csum[-1]
    padded = ((counts + tm - 1) // tm) * tm
    ends = jnp.cumsum(padded)
    starts = ends - padded
    pos = jnp.sum(onehot * starts[None, :], axis=1) + rank
    n_rows = 2 * t + ne * tm
    n_tiles = n_rows // tm
    src = jnp.zeros((n_rows,), jnp.int32).at[pos].set(tok, unique_indices=True)
    tile_start = jnp.arange(n_tiles, dtype=jnp.int32) * tm
    tile_expert = jnp.minimum(jnp.sum((tile_start[:, None] >= ends[None, :]).astype(jnp.int32), axis=1), ne - 1)
    n_used = (ends[-1] // tm).astype(jnp.int32)
    last_expert = tile_expert[jnp.maximum(n_used - 1, 0)]
    tile_expert = jnp.where(jnp.arange(n_tiles) < n_used, tile_expert, last_expert).astype(jnp.int32)
    xs = jnp.take(xbf, src, axis=0)
    ys = _moe_ffn(xs, wg, wu, wd, tile_expert, n_used.reshape(1), tm)
    y1 = jnp.take(ys, pos[:t], axis=0)
    y2 = jnp.take(ys, pos[t:], axis=0)
    wcol = route[2:4].T
    return _combine_ln(x, y1, y2, wcol, gain, bias, alpha)


def _prep_a_w_in(w):
    d = w.shape[0]
    nq = d
    nkv = KV_HEADS * HEAD_DIM
    q = w[:, :nq][:, _pair_cols(nq // HEAD_DIM)]
    k = w[:, nq:nq + nkv][:, _kdup_cols(KV_HEADS)]
    v = w[:, nq + nkv:][:, _vdup_cols(KV_HEADS)]
    return jnp.concatenate([q, k, v], axis=1).astype(BF16)


def _prep_b_w_in(w):
    d = w.shape[0]
    nq = d
    nkv = KV_HEADS * HEAD_DIM
    part = lambda i: w[:, nq + i * nkv:nq + (i + 1) * nkv]
    q = w[:, :nq][:, _pair_cols(nq // HEAD_DIM)]
    kc = part(0)[:, _pair_cols(KV_HEADS)]
    vc = part(1)
    ks = part(2)[:, _kdup_cols(KV_HEADS)]
    vs = part(3)[:, _vdup_cols(KV_HEADS)]
    kw = part(4)[:, _kdup_cols(KV_HEADS)]
    vw = part(5)[:, _vdup_cols(KV_HEADS)]
    gl = w[:, nq + 6 * nkv:]
    per = GROUP * N_GATES
    gcols = []
    for g in range(KV_HEADS):
        gcols.append(jnp.pad(gl[:, g * per:(g + 1) * per], ((0, 0), (0, LANES - per))))
    return jnp.concatenate([q, kc, ks, kw, vc, vs, vw] + gcols, axis=1).astype(BF16)


def _prep_compress(pos, w1, w2, rope_layout):
    hidden = w1.shape[1]
    w1 = w1.reshape(CMP_LEN, HEAD_DIM, hidden)
    zeros = jnp.zeros_like(w1)
    a = jnp.concatenate([w1, zeros], axis=2)
    b = jnp.concatenate([zeros, w1], axis=2)
    natural = jnp.concatenate([a, b], axis=1)
    pos2 = jnp.concatenate([pos, pos], axis=1)
    if rope_layout:
        cols = _pair_cols(2)
        natural, pos2 = natural[:, cols, :], pos2[:, cols]
        out_cols = _kdup_cols(1)
    else:
        out_cols = _vdup_cols(1)
    w2d = w2[:, out_cols]
    z2 = jnp.zeros_like(w2d)
    w2p = jnp.concatenate([jnp.concatenate([w2d, z2], axis=1), jnp.concatenate([z2, w2d], axis=1)], axis=0)
    return pos2.astype(F32), natural.astype(BF16), w2p.astype(BF16)


def _split_router(router):
    rt = router.T
    hi = rt.astype(BF16)
    lo = (rt - hi.astype(F32)).astype(BF16)
    return jnp.concatenate([hi, lo], axis=0)


def _overlap_t(ncp, nsel):
    cs = CMP_STRIDE * np.arange(ncp)
    ce = cs + CMP_LEN
    ss = SEL_BLOCK * np.arange(nsel)
    se = ss + SEL_BLOCK
    ov = np.clip(np.minimum(ce[None, :], se[:, None]) - np.maximum(cs[None, :], ss[:, None]), 0, None)
    return jnp.asarray(ov / CMP_STRIDE, dtype=BF16)


def _key_block_indicator(seq):
    assert seq // SEL_BLOCK <= LANES
    hit = (np.arange(seq) // SEL_BLOCK)[:, None] == np.arange(LANES)[None, :]
    return jnp.asarray(np.where(hit, SEL_MASK, 0.0), dtype=BF16)


def kernel(x, a_w_in, a_w_out, a_sinks, b_w_in, b_w_out, b_cmp_pos_k, b_cmp_pos_v, b_cmp_k_w1, b_cmp_k_w2, b_cmp_v_w1, b_cmp_v_w2, ffn_w_gate, ffn_w_up, ffn_w_down, moe_router, moe_w_gate, moe_w_up, moe_w_down, ln_gain, ln_bias):
    batch, seq, d = x.shape
    depth = ln_gain.shape[0]
    alpha = float((2 * depth) ** 0.25)
    tables = _rope_tables(seq)
    xf = x.reshape(batch * seq, d)
    xbf = xf
    for i in range(depth):
        j = i // 2
        if i % 2 == 0:
            q, k2, v2 = _proj_a(xbf, _prep_a_w_in(a_w_in[j]), tables, seq)
            o = _swa_attention(q, k2, v2, a_sinks[j], batch, seq)
            xf, xbf = _outproj_ln(o, a_w_out[j].astype(BF16), xf, ln_gain[i, 0], ln_bias[i, 0], alpha)
            xf, xbf = _ffn_ln(xbf, xf, ffn_w_gate[j].astype(BF16), ffn_w_up[j].astype(BF16),
                              ffn_w_down[j].astype(BF16), ln_gain[i, 1], ln_bias[i, 1], alpha)
        else:
            q, kc, ks2, kw2, vc, vs2, vw2, gates = _proj_b(xbf, _prep_b_w_in(b_w_in[j]), tables, seq)
            pk, w1k, w2k = _prep_compress(b_cmp_pos_k[j], b_cmp_k_w1[j], b_cmp_k_w2[j], True)
            pv, w1v, w2v = _prep_compress(b_cmp_pos_v[j], b_cmp_v_w1[j], b_cmp_v_w2[j], False)
            kcc, vcc = _compress(kc, vc, pk, pv, w1k, w1v, w2k, w2v, batch, seq)
            ovt = _overlap_t(seq // CMP_STRIDE, seq // SEL_BLOCK)
            o = _nsa_attention(q, gates, kcc, vcc, ks2, vs2, kw2, vw2, ovt,
                               _key_block_indicator(seq), batch, seq)
            xf, xbf, route = _outproj_ln(o, b_w_out[j].astype(BF16), xf, ln_gain[i, 0], ln_bias[i, 0], alpha,
                                         router_t=_split_router(moe_router[j]))
            xf, xbf = _moe(xf, xbf, route, moe_w_gate[j].astype(BF16), moe_w_up[j].astype(BF16),
                           moe_w_down[j].astype(BF16), ln_gain[i, 1], ln_bias[i, 1], alpha)
    return xf.reshape(batch, seq, d)
```

```python
import functools
import math

import numpy as np
import jax
import jax.numpy as jnp
from jax import lax
from jax.experimental import pallas as pl
from jax.experimental.pallas import tpu as pltpu

F32 = jnp.float32
BF16 = jnp.bfloat16

HEAD_DIM = 64
HALF_DIM = HEAD_DIM // 2
LANES = 128
MXU_WIDTH = 256
ROPE_THETA = 10000.0
LOG2E = math.log2(math.e)
Q_SCALE = HEAD_DIM ** -0.5 * LOG2E
KV_HEADS = 4
GROUP = 4
SWA_BLOCK = 128
SWA_WINDOW = 128
CMP_LEN = 32
CMP_STRIDE = 16
SEL_BLOCK = 64
SEL_SHIFT = 6
SEL_TOPN = 16
NSA_WINDOW = 512
SEL_FORCE = 1e4
N_GATES = 3
LN_EPS = 1e-5
NEG_INF = -1e30
SEL_MASK = 2.0 ** 100
VMEM_LIMIT = 48 * 1024 * 1024

NSA_TQ = 128
NSA_KC = 512


def _cparams(*sem):
    return pltpu.CompilerParams(dimension_semantics=sem, vmem_limit_bytes=VMEM_LIMIT)


def _dot(a, b):
    return jnp.dot(a, b, preferred_element_type=F32)


def _dot_nt(a, b):
    return lax.dot_general(a, b, (((1,), (1,)), ((), ())), preferred_element_type=F32)


def _dot_tn(a, b):
    return lax.dot_general(a, b, (((0,), (0,)), ((), ())), preferred_element_type=F32)


def _pair_cols(n_heads):
    idx = []
    for j in range(n_heads // 2):
        a, b = 2 * j * HEAD_DIM, (2 * j + 1) * HEAD_DIM
        idx += list(range(a, a + HALF_DIM)) + list(range(b, b + HALF_DIM))
        idx += list(range(a + HALF_DIM, a + HEAD_DIM)) + list(range(b + HALF_DIM, b + HEAD_DIM))
    return np.asarray(idx, np.int32)


def _kdup_cols(n_heads):
    idx = []
    for h in range(n_heads):
        a = h * HEAD_DIM
        idx += list(range(a, a + HALF_DIM)) * 2 + list(range(a + HALF_DIM, a + HEAD_DIM)) * 2
    return np.asarray(idx, np.int32)


def _vdup_cols(n_heads):
    idx = []
    for h in range(n_heads):
        idx += list(range(h * HEAD_DIM, (h + 1) * HEAD_DIM)) * 2
    return np.asarray(idx, np.int32)


def _rope_tables(seq):
    inv = 1.0 / (ROPE_THETA ** (jnp.arange(0, HEAD_DIM, 2, dtype=F32) / HEAD_DIM))
    ang = jnp.arange(seq, dtype=F32)[:, None] * inv[None, :]
    cos, sin = jnp.cos(ang), jnp.sin(ang)
    c, s = jnp.tile(cos, (1, 4)), jnp.concatenate([-sin, -sin, sin, sin], axis=1)
    return jnp.concatenate([c * Q_SCALE, s * Q_SCALE, c, s], axis=1)


def _rope(y, c, s):
    return y * c + pltpu.roll(y, LANES // 2, 1) * s


def _project_rope(x, w_ref, col, out_ref, c, s):
    width = out_ref.shape[1]
    step = min(MXU_WIDTH, width)
    for j in range(width // step):
        y = _dot(x, w_ref[:, col + j * step:col + (j + 1) * step])
        for i in range(step // LANES):
            lo = j * step + i * LANES
            out_ref[:, lo:lo + LANES] = _rope(y[:, i * LANES:(i + 1) * LANES], c, s).astype(out_ref.dtype)


def _split_tables(tab_ref):
    tab = tab_ref[...]
    return [tab[:, i * LANES:(i + 1) * LANES] for i in range(4)]


def _layer_norm(z, g, b):
    mu = jnp.mean(z, axis=-1, keepdims=True)
    zc = z - mu
    var = jnp.mean(zc * zc, axis=-1, keepdims=True)
    return zc * lax.rsqrt(var + LN_EPS) * g + b


def _proj_a_kernel(x_ref, w_ref, tab_ref, q_ref, k_ref, v_ref):
    x = x_ref[...].astype(BF16)
    cq, sq, c, s = _split_tables(tab_ref)
    nq = q_ref.shape[1]
    nk = k_ref.shape[1]
    _project_rope(x, w_ref, 0, q_ref, cq, sq)
    _project_rope(x, w_ref, nq, k_ref, c, s)
    v_ref[...] = _dot(x, w_ref[:, nq + nk:]).astype(BF16)


def _proj_a(x2d, w, tables, seq, tm=512):
    t, d = x2d.shape
    nq = d
    nk = KV_HEADS * LANES
    per_seq = seq // tm
    return pl.pallas_call(
        _proj_a_kernel,
        grid=(t // tm,),
        in_specs=[
            pl.BlockSpec((tm, d), lambda i: (i, 0)),
            pl.BlockSpec(w.shape, lambda i: (0, 0)),
            pl.BlockSpec((tm, 4 * LANES), lambda i: (i % per_seq, 0)),
        ],
        out_specs=[
            pl.BlockSpec((tm, nq), lambda i: (i, 0)),
            pl.BlockSpec((tm, nk), lambda i: (i, 0)),
            pl.BlockSpec((tm, nk), lambda i: (i, 0)),
        ],
        out_shape=[
            jax.ShapeDtypeStruct((t, nq), BF16),
            jax.ShapeDtypeStruct((t, nk), BF16),
            jax.ShapeDtypeStruct((t, nk), BF16),
        ],
        compiler_params=_cparams("parallel"),
        name="proj_a",
    )(x2d, w, tables)


def _proj_b_kernel(x_ref, w_ref, tab_ref,
                   q_ref, kc_ref, ks_ref, kw_ref, vc_ref, vs_ref, vw_ref, g_ref):
    x = x_ref[...].astype(BF16)
    cq, sq, c, s = _split_tables(tab_ref)
    col = 0
    for ref in (q_ref, kc_ref, ks_ref, kw_ref):
        cr, sr = (cq, sq) if ref is q_ref else (c, s)
        _project_rope(x, w_ref, col, ref, cr, sr)
        col += ref.shape[1]
    for ref in (vc_ref, vs_ref, vw_ref):
        n = ref.shape[1]
        ref[...] = _dot(x, w_ref[:, col:col + n]).astype(ref.dtype)
        col += n
    g_ref[...] = jax.nn.sigmoid(_dot(x, w_ref[:, col:]))


def _proj_b(x2d, w, tables, seq, tm=512):
    t, d = x2d.shape
    nkv = KV_HEADS * HEAD_DIM
    ndup = KV_HEADS * LANES
    widths = [(d, BF16), (nkv, F32), (ndup, BF16), (ndup, BF16),
              (nkv, F32), (ndup, BF16), (ndup, BF16), (KV_HEADS * LANES, F32)]
    per_seq = seq // tm
    return pl.pallas_call(
        _proj_b_kernel,
        grid=(t // tm,),
        in_specs=[
            pl.BlockSpec((tm, d), lambda i: (i, 0)),
            pl.BlockSpec(w.shape, lambda i: (0, 0)),
            pl.BlockSpec((tm, 4 * LANES), lambda i: (i % per_seq, 0)),
        ],
        out_specs=[pl.BlockSpec((tm, n), lambda i: (i, 0)) for n, _ in widths],
        out_shape=[jax.ShapeDtypeStruct((t, n), dt) for n, dt in widths],
        compiler_params=_cparams("parallel"),
        name="proj_b",
    )(x2d, w, tables)


def _stack_heads(q_ref, rows):
    lane = lax.broadcasted_iota(jnp.int32, (rows, LANES), 1)
    first = (lane & (HEAD_DIM - 1)) < HALF_DIM
    zero = jnp.zeros((rows, LANES), BF16)
    parts = []
    for p in range(GROUP // 2):
        qp = q_ref[:, p * LANES:(p + 1) * LANES]
        parts += [jnp.where(first, qp, zero), jnp.where(first, zero, qp)]
    return jnp.concatenate(parts, axis=0)


def _unstack_heads(o, rows):
    lane = lax.broadcasted_iota(jnp.int32, (rows, LANES), 1)
    low = lane < HEAD_DIM
    return [jnp.where(low, o[2 * p], o[2 * p + 1]) for p in range(GROUP // 2)]


def _swa_kernel(sink_ref, q_ref, kp_ref, kc_ref, vp_ref, vc_ref, o_ref):
    n = pl.program_id(1)
    blk = SWA_BLOCK
    qi = lax.broadcasted_iota(jnp.int32, (blk, 2 * blk), 0)
    si = lax.broadcasted_iota(jnp.int32, (blk, 2 * blk), 1)
    diff = qi + blk - si
    ok = (diff >= 0) & (diff < SWA_WINDOW) & ((si >= blk) | (n > 0))
    for g in range(KV_HEADS):
        q4 = _stack_heads(q_ref.at[:, g * 2 * LANES:(g + 1) * 2 * LANES], blk)
        kk = jnp.concatenate([kp_ref[:, g * LANES:(g + 1) * LANES], kc_ref[:, g * LANES:(g + 1) * LANES]], axis=0)
        vv = jnp.concatenate([vp_ref[:, g * LANES:(g + 1) * LANES], vc_ref[:, g * LANES:(g + 1) * LANES]], axis=0)
        s = _dot_nt(q4, kk).reshape(GROUP, blk, 2 * blk)
        s = jnp.where(ok[None], s, NEG_INF)
        ps, rs = [], []
        for h in range(GROUP):
            sink = sink_ref[g * GROUP + h] * LOG2E
            m = jnp.maximum(jnp.max(s[h], axis=-1, keepdims=True), sink)
            p = jnp.exp2(s[h] - m)
            rs.append(1.0 / (jnp.sum(p, axis=-1, keepdims=True) + jnp.exp2(sink - m)))
            ps.append(p.astype(BF16))
        o = _dot(jnp.concatenate(ps, axis=0), vv).reshape(GROUP, blk, LANES)
        o = [o[h] * rs[h] for h in range(GROUP)]
        for p, blkout in enumerate(_unstack_heads(o, blk)):
            c0 = (g * 2 + p) * LANES
            o_ref[:, c0:c0 + LANES] = blkout.astype(BF16)


def _swa_attention(q, k2, v2, sinks, batch, seq):
    t, nq = q.shape
    nk = k2.shape[1]
    nb = seq // SWA_BLOCK
    cur = lambda b, n: (b * nb + n, 0)
    prev = lambda b, n: (b * nb + jnp.maximum(n - 1, 0), 0)
    return pl.pallas_call(
        _swa_kernel,
        grid=(batch, nb),
        in_specs=[
            pl.BlockSpec(memory_space=pltpu.SMEM),
            pl.BlockSpec((SWA_BLOCK, nq), cur),
            pl.BlockSpec((SWA_BLOCK, nk), prev),
            pl.BlockSpec((SWA_BLOCK, nk), cur),
            pl.BlockSpec((SWA_BLOCK, nk), prev),
            pl.BlockSpec((SWA_BLOCK, nk), cur),
        ],
        out_specs=pl.BlockSpec((SWA_BLOCK, nq), cur),
        out_shape=jax.ShapeDtypeStruct((t, nq), BF16),
        compiler_params=_cparams("parallel", "parallel"),
        name="swa_attention",
    )(sinks, q, k2, k2, v2, v2)


def _outproj_ln_kernel(alpha, o_ref, w_ref, x_ref, g_ref, b_ref, y_ref, ybf_ref):
    z = alpha * x_ref[...].astype(F32) + _dot(o_ref[...], w_ref[...])
    y = _layer_norm(z, g_ref[...], b_ref[...])
    y_ref[...] = y
    ybf_ref[...] = y.astype(BF16)


def _outproj_ln_router_kernel(alpha, o_ref, w_ref, x_ref, g_ref, b_ref, r_ref, y_ref, ybf_ref, route_ref):
    z = alpha * x_ref[...].astype(F32) + _dot(o_ref[...], w_ref[...])
    y = _layer_norm(z, g_ref[...], b_ref[...])
    y_ref[...] = y
    ybf_ref[...] = y.astype(BF16)
    ne = r_ref.shape[0] // 2
    y_hi = y.astype(BF16)
    y_lo = (y - y_hi.astype(F32)).astype(BF16)
    part = _dot_nt(r_ref[...], y_hi)
    logits = part[0:ne] + part[ne:2 * ne] + _dot_nt(r_ref[0:ne, :], y_lo)
    eid = lax.broadcasted_iota(jnp.int32, logits.shape, 0)
    m1 = jnp.max(logits, axis=0, keepdims=True)
    i1 = jnp.min(jnp.where(logits == m1, eid, ne), axis=0, keepdims=True)
    rest = jnp.where(eid == i1, -jnp.inf, logits)
    m2 = jnp.max(rest, axis=0, keepdims=True)
    i2 = jnp.min(jnp.where(rest == m2, eid, ne), axis=0, keepdims=True)
    e = jnp.exp(m2 - m1)
    w1 = 1.0 / (1.0 + e)
    w2 = e / (1.0 + e)
    route_ref[...] = jnp.concatenate(
        [i1.astype(F32), i2.astype(F32), w1, w2, jnp.zeros((4, logits.shape[1]), F32)], axis=0)


def _outproj_ln(o, w, x, gain, bias, alpha, router_t=None, tm=512):
    t, d = x.shape
    row = lambda i: (i, 0)
    const = lambda i: (0, 0)
    in_specs = [
        pl.BlockSpec((tm, o.shape[1]), row),
        pl.BlockSpec(w.shape, const),
        pl.BlockSpec((tm, d), row),
        pl.BlockSpec((1, d), const),
        pl.BlockSpec((1, d), const),
    ]
    out_specs = [pl.BlockSpec((tm, d), row), pl.BlockSpec((tm, d), row)]
    out_shape = [jax.ShapeDtypeStruct((t, d), F32), jax.ShapeDtypeStruct((t, d), BF16)]
    args = [o, w, x, gain.reshape(1, d), bias.reshape(1, d)]
    if router_t is None:
        body = functools.partial(_outproj_ln_kernel, alpha)
        name = "outproj_ln"
    else:
        body = functools.partial(_outproj_ln_router_kernel, alpha)
        name = "outproj_ln_router"
        in_specs.append(pl.BlockSpec(router_t.shape, const))
        args.append(router_t)
        out_specs.append(pl.BlockSpec((8, tm), lambda i: (0, i)))
        out_shape.append(jax.ShapeDtypeStruct((8, t), F32))
    return pl.pallas_call(
        body,
        grid=(t // tm,),
        in_specs=in_specs,
        out_specs=out_specs,
        out_shape=out_shape,
        compiler_params=_cparams("parallel"),
        name=name,
    )(*args)


def _ffn_ln_kernel(alpha, xbf_ref, x_ref, wg_ref, wu_ref, wd_ref, g_ref, b_ref, y_ref, ybf_ref, acc_ref):
    j = pl.program_id(1)

    @pl.when(j == 0)
    def _():
        acc_ref[...] = jnp.zeros_like(acc_ref)

    xb = xbf_ref[...]
    h = jax.nn.silu(_dot(xb, wg_ref[...])) * _dot(xb, wu_ref[...])
    acc_ref[...] += _dot(h.astype(BF16), wd_ref[...])

    @pl.when(j == pl.num_programs(1) - 1)
    def _():
        y = _layer_norm(alpha * x_ref[...] + acc_ref[...], g_ref[...], b_ref[...])
        y_ref[...] = y
        ybf_ref[...] = y.astype(BF16)


def _ffn_ln(xbf, x, wg, wu, wd, gain, bias, alpha, tm=1024, tf=512):
    t, d = x.shape
    dff = wg.shape[1]
    row = lambda i, j: (i, 0)
    const = lambda i, j: (0, 0)
    return pl.pallas_call(
        functools.partial(_ffn_ln_kernel, alpha),
        grid=(t // tm, dff // tf),
        in_specs=[
            pl.BlockSpec((tm, d), row),
            pl.BlockSpec((tm, d), row),
            pl.BlockSpec((d, tf), lambda i, j: (0, j)),
            pl.BlockSpec((d, tf), lambda i, j: (0, j)),
            pl.BlockSpec((tf, d), lambda i, j: (j, 0)),
            pl.BlockSpec((1, d), const),
            pl.BlockSpec((1, d), const),
        ],
        out_specs=[pl.BlockSpec((tm, d), row), pl.BlockSpec((tm, d), row)],
        out_shape=[jax.ShapeDtypeStruct((t, d), F32), jax.ShapeDtypeStruct((t, d), BF16)],
        scratch_shapes=[pltpu.VMEM((tm, d), F32)],
        compiler_params=_cparams("parallel", "arbitrary"),
        name="ffn_ln",
    )(xbf, x, wg, wu, wd, gain.reshape(1, d), bias.reshape(1, d))


def _gelu_tanh(x):
    return 0.5 * x * (1.0 + jnp.tanh(math.sqrt(2.0 / math.pi) * (x + 0.044715 * (x * x * x))))


def _compress_kernel(kc_ref, vc_ref, pk_ref, pv_ref, w1k_ref, w1v_ref, w2k_ref, w2v_ref, ko_ref, vo_ref):
    nseg = ko_ref.shape[0]
    half = CMP_LEN // 2

    def one(t_ref, pos_ref, w1_ref, w2_ref, out_ref):
        hidden = w1_ref.shape[2]
        lo = jnp.zeros((nseg, hidden), F32)
        hi = jnp.zeros((nseg, hidden), F32)
        for l in range(half):
            rows = t_ref[pl.ds(l, nseg, stride=CMP_STRIDE), :]
            lo += _dot((rows + pos_ref[l:l + 1, :]).astype(BF16), w1_ref[l])
            hi += _dot((rows + pos_ref[half + l:half + l + 1, :]).astype(BF16), w1_ref[half + l])
        h = _gelu_tanh(lo + pltpu.roll(hi, nseg - 1, 0))
        out_ref[...] = _dot(h.astype(BF16), w2_ref[...]).astype(out_ref.dtype)

    one(kc_ref, pk_ref, w1k_ref, w2k_ref, ko_ref)
    one(vc_ref, pv_ref, w1v_ref, w2v_ref, vo_ref)


def _compress(kc, vc, pk, pv, w1k, w1v, w2k, w2v, batch, seq):
    nseg = seq // CMP_STRIDE
    npair = KV_HEADS // 2
    tok = lambda b, p: (b, p)
    c2 = lambda b, p: (0, 0)
    c3 = lambda b, p: (0, 0, 0)
    return pl.pallas_call(
        _compress_kernel,
        grid=(batch, npair),
        in_specs=[
            pl.BlockSpec((seq, LANES), tok),
            pl.BlockSpec((seq, LANES), tok),
            pl.BlockSpec(pk.shape, c2),
            pl.BlockSpec(pv.shape, c2),
            pl.BlockSpec(w1k.shape, c3),
            pl.BlockSpec(w1v.shape, c3),
            pl.BlockSpec(w2k.shape, c2),
            pl.BlockSpec(w2v.shape, c2),
        ],
        out_specs=[pl.BlockSpec((nseg, 2 * LANES), tok), pl.BlockSpec((nseg, 2 * LANES), tok)],
        out_shape=[jax.ShapeDtypeStruct((batch * nseg, KV_HEADS * LANES), BF16)] * 2,
        compiler_params=_cparams("parallel", "parallel"),
        name="nsa_compress",
    )(kc, vc, pk, pv, w1k, w1v, w2k, w2v)


def _lane_fold(x, op):
    out = x[:, 0:LANES]
    for j in range(1, x.shape[1] // LANES):
        out = op(out, x[:, j * LANES:(j + 1) * LANES])
    return out


def _nsa_kernel(q_ref, gate_ref, kcc_ref, vcc_ref, ks_ref, vs_ref, kw_ref, vw_ref, ovt_ref, blk_ref,
                o_ref, qa_ref, imp_ref, s_ref, mx_ref, l_ref, acc_ref, part_ref):
    tq, kc = NSA_TQ, NSA_KC
    rows = GROUP * tq
    ncp = kcc_ref.shape[0]
    nsel = ovt_ref.shape[0]
    topn = min(SEL_TOPN, nsel)
    t0 = pl.program_id(2) * tq
    qa_ref[:, 0:LANES] = _stack_heads(q_ref, tq)
    tpos = t0 + lax.broadcasted_iota(jnp.int32, (tq, 1), 0)
    gates = gate_ref[...]

    def gate_column(branch):
        return jnp.concatenate([gates[:, N_GATES * h + branch:N_GATES * h + branch + 1] for h in range(GROUP)],
                               axis=0)

    cend = CMP_STRIDE * lax.broadcasted_iota(jnp.int32, (1, ncp), 1) + (CMP_LEN - 1)
    ok_c = (cend <= tpos)[None]
    s = jnp.where(ok_c, _dot_nt(qa_ref[:, 0:LANES], kcc_ref[...]).reshape(GROUP, tq, ncp), NEG_INF)
    p = jnp.where(ok_c, jnp.exp2(s - jnp.max(s, axis=-1, keepdims=True)), 0.0)
    p = p * (1.0 / jnp.maximum(jnp.sum(p, axis=-1, keepdims=True), 1e-30))
    o_c = _dot(p.reshape(rows, ncp).astype(BF16), vcc_ref[...])
    psum = p[0] + p[1] + p[2] + p[3]

    span = NSA_WINDOW + tq
    w0 = pl.multiple_of(jnp.maximum(t0 - NSA_WINDOW, 0), LANES)
    d = tpos - (w0 + lax.broadcasted_iota(jnp.int32, (1, span), 1))
    bias = jnp.where((d >= 0) & (d < NSA_WINDOW), 0.0, NEG_INF)
    s = _dot_nt(qa_ref[:, 0:LANES], kw_ref[pl.ds(w0, span), :])
    s = (s.reshape(GROUP, tq, span) + bias[None]).reshape(rows, span)
    mw = jnp.broadcast_to(jnp.max(_lane_fold(s, jnp.maximum), axis=-1, keepdims=True), (rows, LANES))
    p = [jnp.exp2(s[:, j * LANES:(j + 1) * LANES] - mw) for j in range(span // LANES)]
    lw = jnp.sum(functools.reduce(jnp.add, p), axis=-1, keepdims=True)
    o_w = _dot(jnp.concatenate([x.astype(BF16) for x in p], axis=1), vw_ref[pl.ds(w0, span), :])
    part_ref[...] = gate_column(0) * o_c + (gate_column(2) / lw) * o_w

    ovt = ovt_ref[...]
    imp = jnp.zeros((nsel, tq), F32)
    rem = psum
    for _ in range(3):
        part = rem.astype(BF16)
        imp += _dot_nt(ovt, part)
        rem = rem - part.astype(F32)
    blk = lax.broadcasted_iota(jnp.int32, (nsel, tq), 0)
    cur = jnp.right_shift(t0 + lax.broadcasted_iota(jnp.int32, (nsel, tq), 1), SEL_SHIFT)
    forced = (blk == 0) | (blk == cur) | (blk == cur - 1)
    imp = jnp.where(forced, SEL_FORCE, imp)
    imp = jnp.where(blk <= cur, imp, -1.0)
    imp_ref[...] = imp

    ngrp = nsel // 8
    sub = lax.broadcasted_iota(jnp.int32, (8, tq), 0)
    grp = [imp[8 * r:8 * r + 8] for r in range(ngrp)]
    cnt = [jnp.zeros((8, tq), F32) for _ in range(ngrp)]
    for i in range(nsel):
        row = jnp.broadcast_to(imp_ref[i:i + 1, :], (8, tq))
        for r in range(ngrp):
            ge = jnp.where(row >= grp[r], 1.0, 0.0)
            gt = jnp.where(row > grp[r], 1.0, 0.0)
            if 8 * r > i:
                cnt[r] = cnt[r] + ge
            elif 8 * r + 7 < i:
                cnt[r] = cnt[r] + gt
            else:
                cnt[r] = cnt[r] + jnp.where(sub > i - 8 * r, ge, gt)
    rank = jnp.concatenate(cnt, axis=0)
    unsel = jnp.where((rank < topn) & (blk <= cur), 0.0, -1.0)
    if nsel < LANES:
        unsel = jnp.concatenate([unsel, jnp.zeros((LANES - nsel, tq), F32)], axis=0)
    selq = unsel.T.astype(BF16)
    for h in range(GROUP):
        qa_ref[h * tq:(h + 1) * tq, LANES:2 * LANES] = selq

    def key_side(start):
        return jnp.concatenate([ks_ref[pl.ds(start, kc), :], blk_ref[pl.ds(start, kc), :]], axis=1)

    nfull = t0 // kc
    dstart = pl.multiple_of(nfull * kc, kc)
    causal = jnp.where(dstart + lax.broadcasted_iota(jnp.int32, (1, kc), 1) <= tpos, 0.0, NEG_INF)
    s = (_dot_nt(qa_ref[...], key_side(dstart)).reshape(GROUP, tq, kc) + causal[None]).reshape(rows, kc)
    s_ref[nfull] = s
    mx_ref[...] = _lane_fold(s, jnp.maximum)

    def score_chunks(first, count):
        mx = mx_ref[...]
        for u in range(count):
            s = _dot_nt(qa_ref[...], key_side(pl.multiple_of((first + u) * kc, kc)))
            s_ref[first + u] = s
            mx = jnp.maximum(mx, _lane_fold(s, jnp.maximum))
        mx_ref[...] = mx

    def score_pair(c, carry):
        score_chunks(2 * c, 2)
        return carry

    lax.fori_loop(0, nfull // 2, score_pair, 0)

    @pl.when(nfull % 2 == 1)
    def _():
        score_chunks(nfull - 1, 1)

    mx_ref[...] = jnp.broadcast_to(jnp.max(mx_ref[...], axis=-1, keepdims=True), (rows, LANES))
    l_ref[...] = jnp.zeros(l_ref.shape, F32)
    acc_ref[...] = jnp.zeros(acc_ref.shape, F32)

    def value_chunks(first, count):
        mrep = mx_ref[...]
        lsum = l_ref[...]
        acc = acc_ref[...]
        for u in range(count):
            sc = s_ref[first + u]
            p = [jnp.exp2(sc[:, j * LANES:(j + 1) * LANES] - mrep) for j in range(kc // LANES)]
            lsum = lsum + functools.reduce(jnp.add, p)
            pb = jnp.concatenate([x.astype(BF16) for x in p], axis=1)
            acc = acc + _dot(pb, vs_ref[pl.ds(pl.multiple_of((first + u) * kc, kc), kc), :])
        l_ref[...] = lsum
        acc_ref[...] = acc

    def value_pair(c, carry):
        value_chunks(2 * c, 2)
        return carry

    lax.fori_loop(0, (nfull + 1) // 2, value_pair, 0)

    @pl.when(nfull % 2 == 0)
    def _():
        value_chunks(nfull, 1)

    ls = jnp.sum(l_ref[...], axis=-1, keepdims=True)
    out = part_ref[...] + (gate_column(1) / ls) * acc_ref[...]
    for pidx, blkout in enumerate(_unstack_heads(out.reshape(GROUP, tq, LANES), tq)):
        o_ref[:, pidx * LANES:(pidx + 1) * LANES] = blkout.astype(BF16)


def _nsa_attention(q, gates, kcc, vcc, ks2, vs2, kw2, vw2, ovt, key_blk, batch, seq):
    t, nq = q.shape
    tq = NSA_TQ
    rows = GROUP * tq
    nt = seq // tq
    ncp = seq // CMP_STRIDE
    nsel = seq // SEL_BLOCK
    qmap = lambda b, g, i: (b * nt + i, g)
    kvmap = lambda b, g, i: (b, g)
    return pl.pallas_call(
        _nsa_kernel,
        grid=(batch, KV_HEADS, nt),
        in_specs=[
            pl.BlockSpec((tq, 2 * LANES), qmap),
            pl.BlockSpec((tq, LANES), qmap),
            pl.BlockSpec((ncp, LANES), kvmap),
            pl.BlockSpec((ncp, LANES), kvmap),
            pl.BlockSpec((seq, LANES), kvmap),
            pl.BlockSpec((seq, LANES), kvmap),
            pl.BlockSpec((seq, LANES), kvmap),
            pl.BlockSpec((seq, LANES), kvmap),
            pl.BlockSpec(ovt.shape, lambda b, g, i: (0, 0)),
            pl.BlockSpec(key_blk.shape, lambda b, g, i: (0, 0)),
        ],
        out_specs=pl.BlockSpec((tq, 2 * LANES), qmap),
        out_shape=jax.ShapeDtypeStruct((t, nq), BF16),
        scratch_shapes=[
            pltpu.VMEM((rows, 2 * LANES), BF16),
            pltpu.VMEM((nsel, tq), F32),
            pltpu.VMEM((seq // NSA_KC, rows, NSA_KC), F32),
            pltpu.VMEM((rows, LANES), F32),
            pltpu.VMEM((rows, LANES), F32),
            pltpu.VMEM((rows, LANES), F32),
            pltpu.VMEM((rows, LANES), F32),
        ],
        compiler_params=_cparams("parallel", "parallel", "arbitrary"),
        name="nsa_attention",
    )(q, gates, kcc, vcc, ks2, vs2, kw2, vw2, ovt, key_blk)


def _moe_kernel(te_ref, nu_ref, xs_ref, wg_ref, wu_ref, wd_ref, ys_ref, acc_ref):
    i = pl.program_id(0)
    j = pl.program_id(1)

    @pl.when(i < nu_ref[0])
    def _():
        @pl.when(j == 0)
        def _():
            acc_ref[...] = jnp.zeros_like(acc_ref)

        xb = xs_ref[...]
        h = jax.nn.silu(_dot(xb, wg_ref[0].astype(BF16))) * _dot(xb, wu_ref[0].astype(BF16))
        acc_ref[...] += _dot(h.astype(BF16), wd_ref[0].astype(BF16))

        @pl.when(j == pl.num_programs(1) - 1)
        def _():
            ys_ref[...] = acc_ref[...].astype(ys_ref.dtype)

    @pl.when((i >= nu_ref[0]) & (j == pl.num_programs(1) - 1))
    def _():
        ys_ref[...] = jnp.zeros_like(ys_ref)


def _moe_ffn(xs, wg, wu, wd, tile_expert, n_used, tm, tf=512):
    p, d = xs.shape
    dff = wg.shape[2]
    nj = dff // tf

    def rows(i, j, te, nu):
        return (jnp.minimum(i, nu[0] - 1), 0)

    def jj(i, j, nu):
        return jnp.where(i < nu[0], j, nj - 1)

    return pl.pallas_call(
        _moe_kernel,
        grid_spec=pltpu.PrefetchScalarGridSpec(
            num_scalar_prefetch=2,
            grid=(p // tm, nj),
            in_specs=[
                pl.BlockSpec((tm, d), rows),
                pl.BlockSpec((1, d, tf), lambda i, j, te, nu: (te[i], 0, jj(i, j, nu))),
                pl.BlockSpec((1, d, tf), lambda i, j, te, nu: (te[i], 0, jj(i, j, nu))),
                pl.BlockSpec((1, tf, d), lambda i, j, te, nu: (te[i], jj(i, j, nu), 0)),
            ],
            out_specs=pl.BlockSpec((tm, d), lambda i, j, te, nu: (i, 0)),
            scratch_shapes=[pltpu.VMEM((tm, d), F32)],
        ),
        out_shape=jax.ShapeDtypeStruct((p, d), BF16),
        compiler_params=_cparams("arbitrary", "arbitrary"),
        name="moe_ffn",
    )(tile_expert, n_used, xs, wg, wu, wd)


def _combine_ln_kernel(alpha, x_ref, y1_ref, y2_ref, w_ref, g_ref, b_ref, o_ref, obf_ref):
    w = w_ref[...]
    f = w[:, 0:1] * y1_ref[...].astype(F32) + w[:, 1:2] * y2_ref[...].astype(F32)
    y = _layer_norm(alpha * x_ref[...] + f, g_ref[...], b_ref[...])
    o_ref[...] = y
    obf_ref[...] = y.astype(BF16)


def _combine_ln(x, y1, y2, wcol, gain, bias, alpha, tm=512):
    t, d = x.shape
    row = lambda i: (i, 0)
    const = lambda i: (0, 0)
    return pl.pallas_call(
        functools.partial(_combine_ln_kernel, alpha),
        grid=(t // tm,),
        in_specs=[
            pl.BlockSpec((tm, d), row),
            pl.BlockSpec((tm, d), row),
            pl.BlockSpec((tm, d), row),
            pl.BlockSpec((tm, 2), row),
            pl.BlockSpec((1, d), const),
            pl.BlockSpec((1, d), const),
        ],
        out_specs=[pl.BlockSpec((tm, d), row), pl.BlockSpec((tm, d), row)],
        out_shape=[jax.ShapeDtypeStruct((t, d), F32), jax.ShapeDtypeStruct((t, d), BF16)],
        compiler_params=_cparams("parallel"),
        name="combine_ln",
    )(x, y1, y2, wcol, gain.reshape(1, d), bias.reshape(1, d))


def _moe(x, xbf, route, wg, wu, wd, gain, bias, alpha, tm=1024):
    t, d = x.shape
    ne = wg.shape[0]
    ids = route[0:2].astype(jnp.int32)
    experts = jnp.arange(ne, dtype=jnp.int32)[None, :]
    hit1 = ids[0][:, None] == experts
    hit2 = ids[1][:, None] == experts
    csum = jnp.cumsum((hit1 | hit2).astype(jnp.int32), axis=0)
    counts = csum[-1]
    padded = ((counts + tm - 1) // tm) * tm
    ends = jnp.cumsum(padded)
    starts = ends - padded
    slot = starts[None, :] + csum - 1
    pos = jnp.concatenate([jnp.sum(jnp.where(hit1, slot, 0), axis=1), jnp.sum(jnp.where(hit2, slot, 0), axis=1)])
    n_rows = 2 * t + ne * tm
    n_tiles = n_rows // tm
    tile_start = jnp.arange(n_tiles, dtype=jnp.int32) * tm
    tile_expert = jnp.minimum(jnp.sum((tile_start[:, None] >= ends[None, :]).astype(jnp.int32), axis=1), ne - 1)
    n_used = (ends[-1] // tm).astype(jnp.int32)
    tok = jnp.tile(jnp.arange(t, dtype=jnp.int32), 2)
    _, tok_sorted = lax.sort_key_val(pos, tok)
    tok_sorted = jnp.concatenate([tok_sorted, jnp.zeros((tm,), jnp.int32)])
    first = (jnp.cumsum(counts) - counts)[tile_expert] + tile_start - starts[tile_expert]
    first = jnp.clip(first, 0, 2 * t)
    src = jax.vmap(lambda f: lax.dynamic_slice(tok_sorted, (f,), (tm,)))(first).reshape(n_rows)
    last_expert = tile_expert[jnp.maximum(n_used - 1, 0)]
    tile_expert = jnp.where(jnp.arange(n_tiles) < n_used, tile_expert, last_expert).astype(jnp.int32)
    xs = jnp.take(xbf, src, axis=0)
    ys = _moe_ffn(xs, wg, wu, wd, tile_expert, n_used.reshape(1), tm)
    y1 = jnp.take(ys, pos[:t], axis=0)
    y2 = jnp.take(ys, pos[t:], axis=0)
    wcol = route[2:4].T
    return _combine_ln(x, y1, y2, wcol, gain, bias, alpha)


def _prep_a_w_in(w):
    d = w.shape[0]
    nq = d
    nkv = KV_HEADS * HEAD_DIM
    q = w[:, :nq][:, _pair_cols(nq // HEAD_DIM)]
    k = w[:, nq:nq + nkv][:, _kdup_cols(KV_HEADS)]
    v = w[:, nq + nkv:][:, _vdup_cols(KV_HEADS)]
    return jnp.concatenate([q, k, v], axis=1).astype(BF16)


def _prep_b_w_in(w):
    d = w.shape[0]
    nq = d
    nkv = KV_HEADS * HEAD_DIM
    part = lambda i: w[:, nq + i * nkv:nq + (i + 1) * nkv]
    q = w[:, :nq][:, _pair_cols(nq // HEAD_DIM)]
    kc = part(0)[:, _pair_cols(KV_HEADS)]
    vc = part(1)
    ks = part(2)[:, _kdup_cols(KV_HEADS)]
    vs = part(3)[:, _vdup_cols(KV_HEADS)]
    kw = part(4)[:, _kdup_cols(KV_HEADS)]
    vw = part(5)[:, _vdup_cols(KV_HEADS)]
    gl = w[:, nq + 6 * nkv:]
    per = GROUP * N_GATES
    gcols = []
    for g in range(KV_HEADS):
        gcols.append(jnp.pad(gl[:, g * per:(g + 1) * per], ((0, 0), (0, LANES - per))))
    return jnp.concatenate([q, kc, ks, kw, vc, vs, vw] + gcols, axis=1).astype(BF16)


def _prep_compress(pos, w1, w2, rope_layout):
    hidden = w1.shape[1]
    w1 = w1.reshape(CMP_LEN, HEAD_DIM, hidden)
    zeros = jnp.zeros_like(w1)
    a = jnp.concatenate([w1, zeros], axis=2)
    b = jnp.concatenate([zeros, w1], axis=2)
    natural = jnp.concatenate([a, b], axis=1)
    pos2 = jnp.concatenate([pos, pos], axis=1)
    if rope_layout:
        cols = _pair_cols(2)
        natural, pos2 = natural[:, cols, :], pos2[:, cols]
        out_cols = _kdup_cols(1)
    else:
        out_cols = _vdup_cols(1)
    w2d = w2[:, out_cols]
    z2 = jnp.zeros_like(w2d)
    w2p = jnp.concatenate([jnp.concatenate([w2d, z2], axis=1), jnp.concatenate([z2, w2d], axis=1)], axis=0)
    return pos2.astype(F32), natural.astype(BF16), w2p.astype(BF16)


def _split_router(router):
    rt = router.T
    hi = rt.astype(BF16)
    lo = (rt - hi.astype(F32)).astype(BF16)
    return jnp.concatenate([hi, lo], axis=0)


def _overlap_t(ncp, nsel):
    cs = CMP_STRIDE * np.arange(ncp)
    ce = cs + CMP_LEN
    ss = SEL_BLOCK * np.arange(nsel)
    se = ss + SEL_BLOCK
    ov = np.clip(np.minimum(ce[None, :], se[:, None]) - np.maximum(cs[None, :], ss[:, None]), 0, None)
    return jnp.asarray(ov / CMP_STRIDE, dtype=BF16)


def _key_block_indicator(seq):
    assert seq // SEL_BLOCK <= LANES
    hit = (np.arange(seq) // SEL_BLOCK)[:, None] == np.arange(LANES)[None, :]
    return jnp.asarray(np.where(hit, SEL_MASK, 0.0), dtype=BF16)


def kernel(x, a_w_in, a_w_out, a_sinks, b_w_in, b_w_out, b_cmp_pos_k, b_cmp_pos_v, b_cmp_k_w1, b_cmp_k_w2, b_cmp_v_w1, b_cmp_v_w2, ffn_w_gate, ffn_w_up, ffn_w_down, moe_router, moe_w_gate, moe_w_up, moe_w_down, ln_gain, ln_bias):
    batch, seq, d = x.shape
    depth = ln_gain.shape[0]
    alpha = float((2 * depth) ** 0.25)
    tables = _rope_tables(seq)
    xf = x.reshape(batch * seq, d)
    xbf = xf
    for i in range(depth):
        j = i // 2
        if i % 2 == 0:
            q, k2, v2 = _proj_a(xbf, _prep_a_w_in(a_w_in[j]), tables, seq)
            o = _swa_attention(q, k2, v2, a_sinks[j], batch, seq)
            xf, xbf = _outproj_ln(o, a_w_out[j].astype(BF16), xf, ln_gain[i, 0], ln_bias[i, 0], alpha)
            xf, xbf = _ffn_ln(xbf, xf, ffn_w_gate[j].astype(BF16), ffn_w_up[j].astype(BF16),
                              ffn_w_down[j].astype(BF16), ln_gain[i, 1], ln_bias[i, 1], alpha)
        else:
            q, kc, ks2, kw2, vc, vs2, vw2, gates = _proj_b(xbf, _prep_b_w_in(b_w_in[j]), tables, seq)
            pk, w1k, w2k = _prep_compress(b_cmp_pos_k[j], b_cmp_k_w1[j], b_cmp_k_w2[j], True)
            pv, w1v, w2v = _prep_compress(b_cmp_pos_v[j], b_cmp_v_w1[j], b_cmp_v_w2[j], False)
            kcc, vcc = _compress(kc, vc, pk, pv, w1k, w1v, w2k, w2v, batch, seq)
            ovt = _overlap_t(seq // CMP_STRIDE, seq // SEL_BLOCK)
            o = _nsa_attention(q, gates, kcc, vcc, ks2, vs2, kw2, vw2, ovt,
                               _key_block_indicator(seq), batch, seq)
            xf, xbf, route = _outproj_ln(o, b_w_out[j].astype(BF16), xf, ln_gain[i, 0], ln_bias[i, 0], alpha,
                                         router_t=_split_router(moe_router[j]))
            xf, xbf = _moe(xf, xbf, route, moe_w_gate[j], moe_w_up[j], moe_w_down[j],
                           ln_gain[i, 1], ln_bias[i, 1], alpha)
    return xf.reshape(batch, seq, d)
```

```python
import functools
import math

import numpy as np
import jax
import jax.numpy as jnp
from jax import lax
from jax.experimental import pallas as pl
from jax.experimental.pallas import tpu as pltpu

F32 = jnp.float32
BF16 = jnp.bfloat16

HEAD_DIM = 64
HALF_DIM = HEAD_DIM // 2
LANES = 128
MXU_WIDTH = 256
ROPE_THETA = 10000.0
LOG2E = math.log2(math.e)
Q_SCALE = HEAD_DIM ** -0.5 * LOG2E
KV_HEADS = 4
GROUP = 4
SWA_BLOCK = 128
SWA_WINDOW = 128
CMP_LEN = 32
CMP_STRIDE = 16
SEL_BLOCK = 64
SEL_SHIFT = 6
SEL_TOPN = 16
NSA_WINDOW = 512
SEL_FORCE = 1e4
N_GATES = 3
LN_EPS = 1e-5
NEG_INF = -1e30
SEL_MASK = 2.0 ** 100
VMEM_LIMIT = 48 * 1024 * 1024

NSA_TQ = 256
NSA_KC = 512


def _cparams(*sem):
    return pltpu.CompilerParams(dimension_semantics=sem, vmem_limit_bytes=VMEM_LIMIT)


def _dot(a, b):
    return jnp.dot(a, b, preferred_element_type=F32)


def _dot_nt(a, b):
    return lax.dot_general(a, b, (((1,), (1,)), ((), ())), preferred_element_type=F32)


def _dot_tn(a, b):
    return lax.dot_general(a, b, (((0,), (0,)), ((), ())), preferred_element_type=F32)


def _pair_cols(n_heads):
    idx = []
    for j in range(n_heads // 2):
        a, b = 2 * j * HEAD_DIM, (2 * j + 1) * HEAD_DIM
        idx += list(range(a, a + HALF_DIM)) + list(range(b, b + HALF_DIM))
        idx += list(range(a + HALF_DIM, a + HEAD_DIM)) + list(range(b + HALF_DIM, b + HEAD_DIM))
    return np.asarray(idx, np.int32)


def _kdup_cols(n_heads):
    idx = []
    for h in range(n_heads):
        a = h * HEAD_DIM
        idx += list(range(a, a + HALF_DIM)) * 2 + list(range(a + HALF_DIM, a + HEAD_DIM)) * 2
    return np.asarray(idx, np.int32)


def _vdup_cols(n_heads):
    idx = []
    for h in range(n_heads):
        idx += list(range(h * HEAD_DIM, (h + 1) * HEAD_DIM)) * 2
    return np.asarray(idx, np.int32)


def _rope_tables(seq):
    inv = 1.0 / (ROPE_THETA ** (jnp.arange(0, HEAD_DIM, 2, dtype=F32) / HEAD_DIM))
    ang = jnp.arange(seq, dtype=F32)[:, None] * inv[None, :]
    cos, sin = jnp.cos(ang), jnp.sin(ang)
    c, s = jnp.tile(cos, (1, 4)), jnp.concatenate([-sin, -sin, sin, sin], axis=1)
    return jnp.concatenate([c * Q_SCALE, s * Q_SCALE, c, s], axis=1)


def _rope(y, c, s):
    return y * c + pltpu.roll(y, LANES // 2, 1) * s


def _project_rope(x, w_ref, col, out_ref, c, s):
    width = out_ref.shape[1]
    step = min(MXU_WIDTH, width)
    for j in range(width // step):
        y = _dot(x, w_ref[:, col + j * step:col + (j + 1) * step])
        for i in range(step // LANES):
            lo = j * step + i * LANES
            out_ref[:, lo:lo + LANES] = _rope(y[:, i * LANES:(i + 1) * LANES], c, s).astype(out_ref.dtype)


def _split_tables(tab_ref):
    tab = tab_ref[...]
    return [tab[:, i * LANES:(i + 1) * LANES] for i in range(4)]


def _layer_norm(z, g, b):
    mu = jnp.mean(z, axis=-1, keepdims=True)
    zc = z - mu
    var = jnp.mean(zc * zc, axis=-1, keepdims=True)
    return zc * lax.rsqrt(var + LN_EPS) * g + b


def _proj_a_kernel(x_ref, w_ref, tab_ref, q_ref, k_ref, v_ref):
    x = x_ref[...].astype(BF16)
    cq, sq, c, s = _split_tables(tab_ref)
    nq = q_ref.shape[1]
    nk = k_ref.shape[1]
    _project_rope(x, w_ref, 0, q_ref, cq, sq)
    _project_rope(x, w_ref, nq, k_ref, c, s)
    v_ref[...] = _dot(x, w_ref[:, nq + nk:]).astype(BF16)


def _proj_a(x2d, w, tables, seq, tm=512):
    t, d = x2d.shape
    nq = d
    nk = KV_HEADS * LANES
    per_seq = seq // tm
    return pl.pallas_call(
        _proj_a_kernel,
        grid=(t // tm,),
        in_specs=[
            pl.BlockSpec((tm, d), lambda i: (i, 0)),
            pl.BlockSpec(w.shape, lambda i: (0, 0)),
            pl.BlockSpec((tm, 4 * LANES), lambda i: (i % per_seq, 0)),
        ],
        out_specs=[
            pl.BlockSpec((tm, nq), lambda i: (i, 0)),
            pl.BlockSpec((tm, nk), lambda i: (i, 0)),
            pl.BlockSpec((tm, nk), lambda i: (i, 0)),
        ],
        out_shape=[
            jax.ShapeDtypeStruct((t, nq), BF16),
            jax.ShapeDtypeStruct((t, nk), BF16),
            jax.ShapeDtypeStruct((t, nk), BF16),
        ],
        compiler_params=_cparams("parallel"),
        name="proj_a",
    )(x2d, w, tables)


def _proj_b_kernel(x_ref, w_ref, tab_ref,
                   q_ref, kc_ref, ks_ref, kw_ref, vc_ref, vs_ref, vw_ref, g_ref):
    x = x_ref[...].astype(BF16)
    cq, sq, c, s = _split_tables(tab_ref)
    col = 0
    for ref in (q_ref, kc_ref, ks_ref, kw_ref):
        cr, sr = (cq, sq) if ref is q_ref else (c, s)
        _project_rope(x, w_ref, col, ref, cr, sr)
        col += ref.shape[1]
    for ref in (vc_ref, vs_ref, vw_ref):
        n = ref.shape[1]
        ref[...] = _dot(x, w_ref[:, col:col + n]).astype(ref.dtype)
        col += n
    g_ref[...] = jax.nn.sigmoid(_dot(x, w_ref[:, col:]))


def _proj_b(x2d, w, tables, seq, tm=512):
    t, d = x2d.shape
    nkv = KV_HEADS * HEAD_DIM
    ndup = KV_HEADS * LANES
    widths = [(d, BF16), (nkv, F32), (ndup, BF16), (ndup, BF16),
              (nkv, F32), (ndup, BF16), (ndup, BF16), (KV_HEADS * LANES, F32)]
    per_seq = seq // tm
    return pl.pallas_call(
        _proj_b_kernel,
        grid=(t // tm,),
        in_specs=[
            pl.BlockSpec((tm, d), lambda i: (i, 0)),
            pl.BlockSpec(w.shape, lambda i: (0, 0)),
            pl.BlockSpec((tm, 4 * LANES), lambda i: (i % per_seq, 0)),
        ],
        out_specs=[pl.BlockSpec((tm, n), lambda i: (i, 0)) for n, _ in widths],
        out_shape=[jax.ShapeDtypeStruct((t, n), dt) for n, dt in widths],
        compiler_params=_cparams("parallel"),
        name="proj_b",
    )(x2d, w, tables)


def _stack_heads(q_ref, rows):
    lane = lax.broadcasted_iota(jnp.int32, (rows, LANES), 1)
    first = (lane & (HEAD_DIM - 1)) < HALF_DIM
    zero = jnp.zeros((rows, LANES), BF16)
    parts = []
    for p in range(GROUP // 2):
        qp = q_ref[:, p * LANES:(p + 1) * LANES]
        parts += [jnp.where(first, qp, zero), jnp.where(first, zero, qp)]
    return jnp.concatenate(parts, axis=0)


def _unstack_heads(o, rows):
    lane = lax.broadcasted_iota(jnp.int32, (rows, LANES), 1)
    low = lane < HEAD_DIM
    return [jnp.where(low, o[2 * p], o[2 * p + 1]) for p in range(GROUP // 2)]


def _swa_kernel(sink_ref, q_ref, kp_ref, kc_ref, vp_ref, vc_ref, o_ref):
    n = pl.program_id(1)
    blk = SWA_BLOCK
    qi = lax.broadcasted_iota(jnp.int32, (blk, 2 * blk), 0)
    si = lax.broadcasted_iota(jnp.int32, (blk, 2 * blk), 1)
    diff = qi + blk - si
    ok = (diff >= 0) & (diff < SWA_WINDOW) & ((si >= blk) | (n > 0))
    for g in range(KV_HEADS):
        q4 = _stack_heads(q_ref.at[:, g * 2 * LANES:(g + 1) * 2 * LANES], blk)
        kk = jnp.concatenate([kp_ref[:, g * LANES:(g + 1) * LANES], kc_ref[:, g * LANES:(g + 1) * LANES]], axis=0)
        vv = jnp.concatenate([vp_ref[:, g * LANES:(g + 1) * LANES], vc_ref[:, g * LANES:(g + 1) * LANES]], axis=0)
        s = _dot_nt(q4, kk).reshape(GROUP, blk, 2 * blk)
        s = jnp.where(ok[None], s, NEG_INF)
        ps, rs = [], []
        for h in range(GROUP):
            sink = sink_ref[g * GROUP + h] * LOG2E
            m = jnp.maximum(jnp.max(s[h], axis=-1, keepdims=True), sink)
            p = jnp.exp2(s[h] - m)
            rs.append(1.0 / (jnp.sum(p, axis=-1, keepdims=True) + jnp.exp2(sink - m)))
            ps.append(p.astype(BF16))
        o = _dot(jnp.concatenate(ps, axis=0), vv).reshape(GROUP, blk, LANES)
        o = [o[h] * rs[h] for h in range(GROUP)]
        for p, blkout in enumerate(_unstack_heads(o, blk)):
            c0 = (g * 2 + p) * LANES
            o_ref[:, c0:c0 + LANES] = blkout.astype(BF16)


def _swa_attention(q, k2, v2, sinks, batch, seq):
    t, nq = q.shape
    nk = k2.shape[1]
    nb = seq // SWA_BLOCK
    cur = lambda b, n: (b * nb + n, 0)
    prev = lambda b, n: (b * nb + jnp.maximum(n - 1, 0), 0)
    return pl.pallas_call(
        _swa_kernel,
        grid=(batch, nb),
        in_specs=[
            pl.BlockSpec(memory_space=pltpu.SMEM),
            pl.BlockSpec((SWA_BLOCK, nq), cur),
            pl.BlockSpec((SWA_BLOCK, nk), prev),
            pl.BlockSpec((SWA_BLOCK, nk), cur),
            pl.BlockSpec((SWA_BLOCK, nk), prev),
            pl.BlockSpec((SWA_BLOCK, nk), cur),
        ],
        out_specs=pl.BlockSpec((SWA_BLOCK, nq), cur),
        out_shape=jax.ShapeDtypeStruct((t, nq), BF16),
        compiler_params=_cparams("parallel", "parallel"),
        name="swa_attention",
    )(sinks, q, k2, k2, v2, v2)


def _outproj_ln_kernel(alpha, o_ref, w_ref, x_ref, g_ref, b_ref, y_ref, ybf_ref):
    z = alpha * x_ref[...].astype(F32) + _dot(o_ref[...], w_ref[...])
    y = _layer_norm(z, g_ref[...], b_ref[...])
    y_ref[...] = y
    ybf_ref[...] = y.astype(BF16)


def _outproj_ln_router_kernel(alpha, o_ref, w_ref, x_ref, g_ref, b_ref, r_ref, y_ref, ybf_ref, route_ref):
    z = alpha * x_ref[...].astype(F32) + _dot(o_ref[...], w_ref[...])
    y = _layer_norm(z, g_ref[...], b_ref[...])
    y_ref[...] = y
    ybf_ref[...] = y.astype(BF16)
    ne = r_ref.shape[0] // 2
    y_hi = y.astype(BF16)
    y_lo = (y - y_hi.astype(F32)).astype(BF16)
    part = _dot_nt(r_ref[...], y_hi)
    logits = part[0:ne] + part[ne:2 * ne] + _dot_nt(r_ref[0:ne, :], y_lo)
    eid = lax.broadcasted_iota(jnp.int32, logits.shape, 0)
    m1 = jnp.max(logits, axis=0, keepdims=True)
    i1 = jnp.min(jnp.where(logits == m1, eid, ne), axis=0, keepdims=True)
    rest = jnp.where(eid == i1, -jnp.inf, logits)
    m2 = jnp.max(rest, axis=0, keepdims=True)
    i2 = jnp.min(jnp.where(rest == m2, eid, ne), axis=0, keepdims=True)
    e = jnp.exp(m2 - m1)
    w1 = 1.0 / (1.0 + e)
    w2 = e / (1.0 + e)
    route_ref[...] = jnp.concatenate(
        [i1.astype(F32), i2.astype(F32), w1, w2, jnp.zeros((4, logits.shape[1]), F32)], axis=0)


def _outproj_ln(o, w, x, gain, bias, alpha, router_t=None, tm=512):
    t, d = x.shape
    row = lambda i: (i, 0)
    const = lambda i: (0, 0)
    in_specs = [
        pl.BlockSpec((tm, o.shape[1]), row),
        pl.BlockSpec(w.shape, const),
        pl.BlockSpec((tm, d), row),
        pl.BlockSpec((1, d), const),
        pl.BlockSpec((1, d), const),
    ]
    out_specs = [pl.BlockSpec((tm, d), row), pl.BlockSpec((tm, d), row)]
    out_shape = [jax.ShapeDtypeStruct((t, d), F32), jax.ShapeDtypeStruct((t, d), BF16)]
    args = [o, w, x, gain.reshape(1, d), bias.reshape(1, d)]
    if router_t is None:
        body = functools.partial(_outproj_ln_kernel, alpha)
        name = "outproj_ln"
    else:
        body = functools.partial(_outproj_ln_router_kernel, alpha)
        name = "outproj_ln_router"
        in_specs.append(pl.BlockSpec(router_t.shape, const))
        args.append(router_t)
        out_specs.append(pl.BlockSpec((8, tm), lambda i: (0, i)))
        out_shape.append(jax.ShapeDtypeStruct((8, t), F32))
    return pl.pallas_call(
        body,
        grid=(t // tm,),
        in_specs=in_specs,
        out_specs=out_specs,
        out_shape=out_shape,
        compiler_params=_cparams("parallel"),
        name=name,
    )(*args)


def _ffn_ln_kernel(alpha, xbf_ref, x_ref, wg_ref, wu_ref, wd_ref, g_ref, b_ref, y_ref, ybf_ref, acc_ref):
    j = pl.program_id(1)

    @pl.when(j == 0)
    def _():
        acc_ref[...] = jnp.zeros_like(acc_ref)

    xb = xbf_ref[...]
    h = jax.nn.silu(_dot(xb, wg_ref[...])) * _dot(xb, wu_ref[...])
    acc_ref[...] += _dot(h.astype(BF16), wd_ref[...])

    @pl.when(j == pl.num_programs(1) - 1)
    def _():
        y = _layer_norm(alpha * x_ref[...] + acc_ref[...], g_ref[...], b_ref[...])
        y_ref[...] = y
        ybf_ref[...] = y.astype(BF16)


def _ffn_ln(xbf, x, wg, wu, wd, gain, bias, alpha, tm=1024, tf=512):
    t, d = x.shape
    dff = wg.shape[1]
    row = lambda i, j: (i, 0)
    const = lambda i, j: (0, 0)
    return pl.pallas_call(
        functools.partial(_ffn_ln_kernel, alpha),
        grid=(t // tm, dff // tf),
        in_specs=[
            pl.BlockSpec((tm, d), row),
            pl.BlockSpec((tm, d), row),
            pl.BlockSpec((d, tf), lambda i, j: (0, j)),
            pl.BlockSpec((d, tf), lambda i, j: (0, j)),
            pl.BlockSpec((tf, d), lambda i, j: (j, 0)),
            pl.BlockSpec((1, d), const),
            pl.BlockSpec((1, d), const),
        ],
        out_specs=[pl.BlockSpec((tm, d), row), pl.BlockSpec((tm, d), row)],
        out_shape=[jax.ShapeDtypeStruct((t, d), F32), jax.ShapeDtypeStruct((t, d), BF16)],
        scratch_shapes=[pltpu.VMEM((tm, d), F32)],
        compiler_params=_cparams("parallel", "arbitrary"),
        name="ffn_ln",
    )(xbf, x, wg, wu, wd, gain.reshape(1, d), bias.reshape(1, d))


def _gelu_tanh(x):
    return 0.5 * x * (1.0 + jnp.tanh(math.sqrt(2.0 / math.pi) * (x + 0.044715 * (x * x * x))))


def _compress_kernel(kc_ref, vc_ref, pk_ref, pv_ref, w1k_ref, w1v_ref, w2k_ref, w2v_ref, ko_ref, vo_ref):
    nseg = ko_ref.shape[0]
    half = CMP_LEN // 2

    def one(t_ref, pos_ref, w1_ref, w2_ref, out_ref):
        hidden = w1_ref.shape[2]
        lo = jnp.zeros((nseg, hidden), F32)
        hi = jnp.zeros((nseg, hidden), F32)
        for l in range(half):
            rows = t_ref[pl.ds(l, nseg, stride=CMP_STRIDE), :]
            lo += _dot((rows + pos_ref[l:l + 1, :]).astype(BF16), w1_ref[l])
            hi += _dot((rows + pos_ref[half + l:half + l + 1, :]).astype(BF16), w1_ref[half + l])
        h = _gelu_tanh(lo + pltpu.roll(hi, nseg - 1, 0))
        out_ref[...] = _dot(h.astype(BF16), w2_ref[...]).astype(out_ref.dtype)

    one(kc_ref, pk_ref, w1k_ref, w2k_ref, ko_ref)
    one(vc_ref, pv_ref, w1v_ref, w2v_ref, vo_ref)


def _compress(kc, vc, pk, pv, w1k, w1v, w2k, w2v, batch, seq):
    nseg = seq // CMP_STRIDE
    npair = KV_HEADS // 2
    tok = lambda b, p: (b, p)
    c2 = lambda b, p: (0, 0)
    c3 = lambda b, p: (0, 0, 0)
    return pl.pallas_call(
        _compress_kernel,
        grid=(batch, npair),
        in_specs=[
            pl.BlockSpec((seq, LANES), tok),
            pl.BlockSpec((seq, LANES), tok),
            pl.BlockSpec(pk.shape, c2),
            pl.BlockSpec(pv.shape, c2),
            pl.BlockSpec(w1k.shape, c3),
            pl.BlockSpec(w1v.shape, c3),
            pl.BlockSpec(w2k.shape, c2),
            pl.BlockSpec(w2v.shape, c2),
        ],
        out_specs=[pl.BlockSpec((nseg, 2 * LANES), tok), pl.BlockSpec((nseg, 2 * LANES), tok)],
        out_shape=[jax.ShapeDtypeStruct((batch * nseg, KV_HEADS * LANES), BF16)] * 2,
        compiler_params=_cparams("parallel", "parallel"),
        name="nsa_compress",
    )(kc, vc, pk, pv, w1k, w1v, w2k, w2v)


def _lane_fold(x, op):
    out = x[:, 0:LANES]
    for j in range(1, x.shape[1] // LANES):
        out = op(out, x[:, j * LANES:(j + 1) * LANES])
    return out


def _nsa_kernel(q_ref, gate_ref, kcc_ref, vcc_ref, ks_ref, vs_ref, kw_ref, vw_ref, ovt_ref, blk_ref,
                o_ref, qa_ref, imp_ref, s_ref, mx_ref, l_ref, acc_ref, part_ref):
    tq, kc = NSA_TQ, NSA_KC
    rows = GROUP * tq
    ncp = kcc_ref.shape[0]
    nsel = ovt_ref.shape[0]
    topn = min(SEL_TOPN, nsel)
    t0 = pl.program_id(2) * tq
    qa_ref[:, 0:LANES] = _stack_heads(q_ref, tq)
    tpos = t0 + lax.broadcasted_iota(jnp.int32, (tq, 1), 0)
    gates = gate_ref[...]

    def gate_column(branch):
        return jnp.concatenate([gates[:, N_GATES * h + branch:N_GATES * h + branch + 1] for h in range(GROUP)],
                               axis=0)

    cend = CMP_STRIDE * lax.broadcasted_iota(jnp.int32, (1, ncp), 1) + (CMP_LEN - 1)
    ok_c = (cend <= tpos)[None]
    s = jnp.where(ok_c, _dot_nt(qa_ref[:, 0:LANES], kcc_ref[...]).reshape(GROUP, tq, ncp), NEG_INF)
    p = jnp.where(ok_c, jnp.exp2(s - jnp.max(s, axis=-1, keepdims=True)), 0.0)
    p = p * (1.0 / jnp.maximum(jnp.sum(p, axis=-1, keepdims=True), 1e-30))
    o_c = _dot(p.reshape(rows, ncp).astype(BF16), vcc_ref[...])
    psum = p[0] + p[1] + p[2] + p[3]

    span = NSA_WINDOW + tq
    w0 = pl.multiple_of(jnp.maximum(t0 - NSA_WINDOW, 0), LANES)
    d = tpos - (w0 + lax.broadcasted_iota(jnp.int32, (1, span), 1))
    bias = jnp.where((d >= 0) & (d < NSA_WINDOW), 0.0, NEG_INF)
    s = _dot_nt(qa_ref[:, 0:LANES], kw_ref[pl.ds(w0, span), :])
    s = (s.reshape(GROUP, tq, span) + bias[None]).reshape(rows, span)
    mw = jnp.broadcast_to(jnp.max(_lane_fold(s, jnp.maximum), axis=-1, keepdims=True), (rows, LANES))
    p = [jnp.exp2(s[:, j * LANES:(j + 1) * LANES] - mw) for j in range(span // LANES)]
    lw = jnp.sum(functools.reduce(jnp.add, p), axis=-1, keepdims=True)
    o_w = _dot(jnp.concatenate([x.astype(BF16) for x in p], axis=1), vw_ref[pl.ds(w0, span), :])
    part_ref[...] = gate_column(0) * o_c + (gate_column(2) / lw) * o_w

    ovt = ovt_ref[...]
    imp = jnp.zeros((nsel, tq), F32)
    rem = psum
    for _ in range(3):
        part = rem.astype(BF16)
        imp += _dot_nt(ovt, part)
        rem = rem - part.astype(F32)
    blk = lax.broadcasted_iota(jnp.int32, (nsel, tq), 0)
    cur = jnp.right_shift(t0 + lax.broadcasted_iota(jnp.int32, (nsel, tq), 1), SEL_SHIFT)
    forced = (blk == 0) | (blk == cur) | (blk == cur - 1)
    imp = jnp.where(forced, SEL_FORCE, imp)
    imp = jnp.where(blk <= cur, imp, -1.0)
    imp_ref[...] = imp

    ngrp = nsel // 8
    sub = lax.broadcasted_iota(jnp.int32, (8, tq), 0)
    grp = [imp[8 * r:8 * r + 8] for r in range(ngrp)]
    cnt = [jnp.zeros((8, tq), F32) for _ in range(ngrp)]
    for i in range(nsel):
        row = jnp.broadcast_to(imp_ref[i:i + 1, :], (8, tq))
        for r in range(ngrp):
            ge = jnp.where(row >= grp[r], 1.0, 0.0)
            gt = jnp.where(row > grp[r], 1.0, 0.0)
            if 8 * r > i:
                cnt[r] = cnt[r] + ge
            elif 8 * r + 7 < i:
                cnt[r] = cnt[r] + gt
            else:
                cnt[r] = cnt[r] + jnp.where(sub > i - 8 * r, ge, gt)
    rank = jnp.concatenate(cnt, axis=0)
    unsel = jnp.where((rank < topn) & (blk <= cur), 0.0, -1.0)
    if nsel < LANES:
        unsel = jnp.concatenate([unsel, jnp.zeros((LANES - nsel, tq), F32)], axis=0)
    selq = unsel.T.astype(BF16)
    for h in range(GROUP):
        qa_ref[h * tq:(h + 1) * tq, LANES:2 * LANES] = selq

    def key_side(start):
        return jnp.concatenate([ks_ref[pl.ds(start, kc), :], blk_ref[pl.ds(start, kc), :]], axis=1)

    nfull = t0 // kc
    dstart = pl.multiple_of(nfull * kc, kc)
    causal = jnp.where(dstart + lax.broadcasted_iota(jnp.int32, (1, kc), 1) <= tpos, 0.0, NEG_INF)
    s = (_dot_nt(qa_ref[...], key_side(dstart)).reshape(GROUP, tq, kc) + causal[None]).reshape(rows, kc)
    s_ref[nfull] = s
    mx_ref[...] = _lane_fold(s, jnp.maximum)

    def score_chunks(first, count):
        mx = mx_ref[...]
        for u in range(count):
            s = _dot_nt(qa_ref[...], key_side(pl.multiple_of((first + u) * kc, kc)))
            s_ref[first + u] = s
            mx = jnp.maximum(mx, _lane_fold(s, jnp.maximum))
        mx_ref[...] = mx

    def score_pair(c, carry):
        score_chunks(2 * c, 2)
        return carry

    lax.fori_loop(0, nfull // 2, score_pair, 0)

    @pl.when(nfull % 2 == 1)
    def _():
        score_chunks(nfull - 1, 1)

    mx_ref[...] = jnp.broadcast_to(jnp.max(mx_ref[...], axis=-1, keepdims=True), (rows, LANES))
    l_ref[...] = jnp.zeros(l_ref.shape, F32)
    acc_ref[...] = jnp.zeros(acc_ref.shape, F32)

    def value_chunks(first, count):
        mrep = mx_ref[...]
        lsum = l_ref[...]
        acc = acc_ref[...]
        for u in range(count):
            sc = s_ref[first + u]
            p = [jnp.exp2(sc[:, j * LANES:(j + 1) * LANES] - mrep) for j in range(kc // LANES)]
            lsum = lsum + functools.reduce(jnp.add, p)
            pb = jnp.concatenate([x.astype(BF16) for x in p], axis=1)
            acc = acc + _dot(pb, vs_ref[pl.ds(pl.multiple_of((first + u) * kc, kc), kc), :])
        l_ref[...] = lsum
        acc_ref[...] = acc

    def value_pair(c, carry):
        value_chunks(2 * c, 2)
        return carry

    lax.fori_loop(0, (nfull + 1) // 2, value_pair, 0)

    @pl.when(nfull % 2 == 0)
    def _():
        value_chunks(nfull, 1)

    ls = jnp.sum(l_ref[...], axis=-1, keepdims=True)
    out = part_ref[...] + (gate_column(1) / ls) * acc_ref[...]
    for pidx, blkout in enumerate(_unstack_heads(out.reshape(GROUP, tq, LANES), tq)):
        o_ref[:, pidx * LANES:(pidx + 1) * LANES] = blkout.astype(BF16)


def _nsa_attention(q, gates, kcc, vcc, ks2, vs2, kw2, vw2, ovt, key_blk, batch, seq):
    t, nq = q.shape
    tq = NSA_TQ
    rows = GROUP * tq
    nt = seq // tq
    ncp = seq // CMP_STRIDE
    nsel = seq // SEL_BLOCK
    qmap = lambda b, g, i: (b * nt + i, g)
    kvmap = lambda b, g, i: (b, g)
    return pl.pallas_call(
        _nsa_kernel,
        grid=(batch, KV_HEADS, nt),
        in_specs=[
            pl.BlockSpec((tq, 2 * LANES), qmap),
            pl.BlockSpec((tq, LANES), qmap),
            pl.BlockSpec((ncp, LANES), kvmap),
            pl.BlockSpec((ncp, LANES), kvmap),
            pl.BlockSpec((seq, LANES), kvmap),
            pl.BlockSpec((seq, LANES), kvmap),
            pl.BlockSpec((seq, LANES), kvmap),
            pl.BlockSpec((seq, LANES), kvmap),
            pl.BlockSpec(ovt.shape, lambda b, g, i: (0, 0)),
            pl.BlockSpec(key_blk.shape, lambda b, g, i: (0, 0)),
        ],
        out_specs=pl.BlockSpec((tq, 2 * LANES), qmap),
        out_shape=jax.ShapeDtypeStruct((t, nq), BF16),
        scratch_shapes=[
            pltpu.VMEM((rows, 2 * LANES), BF16),
            pltpu.VMEM((nsel, tq), F32),
            pltpu.VMEM((seq // NSA_KC, rows, NSA_KC), F32),
            pltpu.VMEM((rows, LANES), F32),
            pltpu.VMEM((rows, LANES), F32),
            pltpu.VMEM((rows, LANES), F32),
            pltpu.VMEM((rows, LANES), F32),
        ],
        compiler_params=_cparams("parallel", "parallel", "arbitrary"),
        name="nsa_attention",
    )(q, gates, kcc, vcc, ks2, vs2, kw2, vw2, ovt, key_blk)


def _moe_kernel(te_ref, nu_ref, xs_ref, wg_ref, wu_ref, wd_ref, ys_ref, acc_ref):
    i = pl.program_id(0)
    j = pl.program_id(1)

    @pl.when(i < nu_ref[0])
    def _():
        @pl.when(j == 0)
        def _():
            acc_ref[...] = jnp.zeros_like(acc_ref)

        xb = xs_ref[...]
        h = jax.nn.silu(_dot(xb, wg_ref[0].astype(BF16))) * _dot(xb, wu_ref[0].astype(BF16))
        acc_ref[...] += _dot(h.astype(BF16), wd_ref[0].astype(BF16))

        @pl.when(j == pl.num_programs(1) - 1)
        def _():
            ys_ref[...] = acc_ref[...].astype(ys_ref.dtype)

    @pl.when((i >= nu_ref[0]) & (j == pl.num_programs(1) - 1))
    def _():
        ys_ref[...] = jnp.zeros_like(ys_ref)


def _moe_ffn(xs, wg, wu, wd, tile_expert, n_used, tm, tf=512):
    p, d = xs.shape
    dff = wg.shape[2]
    nj = dff // tf

    def rows(i, j, te, nu):
        return (jnp.minimum(i, nu[0] - 1), 0)

    def jj(i, j, nu):
        return jnp.where(i < nu[0], j, nj - 1)

    return pl.pallas_call(
        _moe_kernel,
        grid_spec=pltpu.PrefetchScalarGridSpec(
            num_scalar_prefetch=2,
            grid=(p // tm, nj),
            in_specs=[
                pl.BlockSpec((tm, d), rows),
                pl.BlockSpec((1, d, tf), lambda i, j, te, nu: (te[i], 0, jj(i, j, nu))),
                pl.BlockSpec((1, d, tf), lambda i, j, te, nu: (te[i], 0, jj(i, j, nu))),
                pl.BlockSpec((1, tf, d), lambda i, j, te, nu: (te[i], jj(i, j, nu), 0)),
            ],
            out_specs=pl.BlockSpec((tm, d), lambda i, j, te, nu: (i, 0)),
            scratch_shapes=[pltpu.VMEM((tm, d), F32)],
        ),
        out_shape=jax.ShapeDtypeStruct((p, d), BF16),
        compiler_params=_cparams("arbitrary", "arbitrary"),
        name="moe_ffn",
    )(tile_expert, n_used, xs, wg, wu, wd)


def _combine_ln_kernel(alpha, x_ref, y1_ref, y2_ref, w_ref, g_ref, b_ref, o_ref, obf_ref):
    w = w_ref[...]
    f = w[:, 0:1] * y1_ref[...].astype(F32) + w[:, 1:2] * y2_ref[...].astype(F32)
    y = _layer_norm(alpha * x_ref[...] + f, g_ref[...], b_ref[...])
    o_ref[...] = y
    obf_ref[...] = y.astype(BF16)


def _combine_ln(x, y1, y2, wcol, gain, bias, alpha, tm=512):
    t, d = x.shape
    row = lambda i: (i, 0)
    const = lambda i: (0, 0)
    return pl.pallas_call(
        functools.partial(_combine_ln_kernel, alpha),
        grid=(t // tm,),
        in_specs=[
            pl.BlockSpec((tm, d), row),
            pl.BlockSpec((tm, d), row),
            pl.BlockSpec((tm, d), row),
            pl.BlockSpec((tm, 2), row),
            pl.BlockSpec((1, d), const),
            pl.BlockSpec((1, d), const),
        ],
        out_specs=[pl.BlockSpec((tm, d), row), pl.BlockSpec((tm, d), row)],
        out_shape=[jax.ShapeDtypeStruct((t, d), F32), jax.ShapeDtypeStruct((t, d), BF16)],
        compiler_params=_cparams("parallel"),
        name="combine_ln",
    )(x, y1, y2, wcol, gain.reshape(1, d), bias.reshape(1, d))


def _moe(x, xbf, route, wg, wu, wd, gain, bias, alpha, tm=1024):
    t, d = x.shape
    ne = wg.shape[0]
    ids = route[0:2].astype(jnp.int32)
    experts = jnp.arange(ne, dtype=jnp.int32)[None, :]
    hit1 = ids[0][:, None] == experts
    hit2 = ids[1][:, None] == experts
    csum = jnp.cumsum((hit1 | hit2).astype(jnp.int32), axis=0)
    counts = csum[-1]
    padded = ((counts + tm - 1) // tm) * tm
    ends = jnp.cumsum(padded)
    starts = ends - padded
    slot = starts[None, :] + csum - 1
    pos = jnp.concatenate([jnp.sum(jnp.where(hit1, slot, 0), axis=1), jnp.sum(jnp.where(hit2, slot, 0), axis=1)])
    n_rows = 2 * t + ne * tm
    n_tiles = n_rows // tm
    tile_start = jnp.arange(n_tiles, dtype=jnp.int32) * tm
    tile_expert = jnp.minimum(jnp.sum((tile_start[:, None] >= ends[None, :]).astype(jnp.int32), axis=1), ne - 1)
    n_used = (ends[-1] // tm).astype(jnp.int32)
    tok = jnp.tile(jnp.arange(t, dtype=jnp.int32), 2)
    _, tok_sorted = lax.sort_key_val(pos, tok)
    tok_sorted = jnp.concatenate([tok_sorted, jnp.arange(tm, dtype=jnp.int32)])
    first = (jnp.cumsum(counts) - counts)[tile_expert] + tile_start - starts[tile_expert]
    first = jnp.clip(first, 0, 2 * t)
    src = jax.vmap(lambda f: lax.dynamic_slice(tok_sorted, (f,), (tm,)))(first).reshape(n_rows)
    last_expert = tile_expert[jnp.maximum(n_used - 1, 0)]
    tile_expert = jnp.where(jnp.arange(n_tiles) < n_used, tile_expert, last_expert).astype(jnp.int32)
    xs = jnp.take(xbf, src, axis=0)
    ys = _moe_ffn(xs, wg, wu, wd, tile_expert, n_used.reshape(1), tm)
    y1 = jnp.take(ys, pos[:t], axis=0)
    y2 = jnp.take(ys, pos[t:], axis=0)
    wcol = route[2:4].T
    return _combine_ln(x, y1, y2, wcol, gain, bias, alpha)


def _prep_a_w_in(w):
    d = w.shape[0]
    nq = d
    nkv = KV_HEADS * HEAD_DIM
    q = w[:, :nq][:, _pair_cols(nq // HEAD_DIM)]
    k = w[:, nq:nq + nkv][:, _kdup_cols(KV_HEADS)]
    v = w[:, nq + nkv:][:, _vdup_cols(KV_HEADS)]
    return jnp.concatenate([q, k, v], axis=1).astype(BF16)


def _prep_b_w_in(w):
    d = w.shape[0]
    nq = d
    nkv = KV_HEADS * HEAD_DIM
    part = lambda i: w[:, nq + i * nkv:nq + (i + 1) * nkv]
    q = w[:, :nq][:, _pair_cols(nq // HEAD_DIM)]
    kc = part(0)[:, _pair_cols(KV_HEADS)]
    vc = part(1)
    ks = part(2)[:, _kdup_cols(KV_HEADS)]
    vs = part(3)[:, _vdup_cols(KV_HEADS)]
    kw = part(4)[:, _kdup_cols(KV_HEADS)]
    vw = part(5)[:, _vdup_cols(KV_HEADS)]
    gl = w[:, nq + 6 * nkv:]
    per = GROUP * N_GATES
    gcols = []
    for g in range(KV_HEADS):
        gcols.append(jnp.pad(gl[:, g * per:(g + 1) * per], ((0, 0), (0, LANES - per))))
    return jnp.concatenate([q, kc, ks, kw, vc, vs, vw] + gcols, axis=1).astype(BF16)


def _prep_compress(pos, w1, w2, rope_layout):
    hidden = w1.shape[1]
    w1 = w1.reshape(CMP_LEN, HEAD_DIM, hidden)
    zeros = jnp.zeros_like(w1)
    a = jnp.concatenate([w1, zeros], axis=2)
    b = jnp.concatenate([zeros, w1], axis=2)
    natural = jnp.concatenate([a, b], axis=1)
    pos2 = jnp.concatenate([pos, pos], axis=1)
    if rope_layout:
        cols = _pair_cols(2)
        natural, pos2 = natural[:, cols, :], pos2[:, cols]
        out_cols = _kdup_cols(1)
    else:
        out_cols = _vdup_cols(1)
    w2d = w2[:, out_cols]
    z2 = jnp.zeros_like(w2d)
    w2p = jnp.concatenate([jnp.concatenate([w2d, z2], axis=1), jnp.concatenate([z2, w2d], axis=1)], axis=0)
    return pos2.astype(F32), natural.astype(BF16), w2p.astype(BF16)


def _split_router(router):
    rt = router.T
    hi = rt.astype(BF16)
    lo = (rt - hi.astype(F32)).astype(BF16)
    return jnp.concatenate([hi, lo], axis=0)


def _overlap_t(ncp, nsel):
    cs = CMP_STRIDE * np.arange(ncp)
    ce = cs + CMP_LEN
    ss = SEL_BLOCK * np.arange(nsel)
    se = ss + SEL_BLOCK
    ov = np.clip(np.minimum(ce[None, :], se[:, None]) - np.maximum(cs[None, :], ss[:, None]), 0, None)
    return jnp.asarray(ov / CMP_STRIDE, dtype=BF16)


def _key_block_indicator(seq):
    assert seq // SEL_BLOCK <= LANES
    hit = (np.arange(seq) // SEL_BLOCK)[:, None] == np.arange(LANES)[None, :]
    return jnp.asarray(np.where(hit, SEL_MASK, 0.0), dtype=BF16)


def kernel(x, a_w_in, a_w_out, a_sinks, b_w_in, b_w_out, b_cmp_pos_k, b_cmp_pos_v, b_cmp_k_w1, b_cmp_k_w2, b_cmp_v_w1, b_cmp_v_w2, ffn_w_gate, ffn_w_up, ffn_w_down, moe_router, moe_w_gate, moe_w_up, moe_w_down, ln_gain, ln_bias):
    batch, seq, d = x.shape
    depth = ln_gain.shape[0]
    alpha = float((2 * depth) ** 0.25)
    tables = _rope_tables(seq)
    xf = x.reshape(batch * seq, d)
    xbf = xf
    for i in range(depth):
        j = i // 2
        if i % 2 == 0:
            q, k2, v2 = _proj_a(xbf, _prep_a_w_in(a_w_in[j]), tables, seq)
            o = _swa_attention(q, k2, v2, a_sinks[j], batch, seq)
            xf, xbf = _outproj_ln(o, a_w_out[j].astype(BF16), xf, ln_gain[i, 0], ln_bias[i, 0], alpha)
            xf, xbf = _ffn_ln(xbf, xf, ffn_w_gate[j].astype(BF16), ffn_w_up[j].astype(BF16),
                              ffn_w_down[j].astype(BF16), ln_gain[i, 1], ln_bias[i, 1], alpha)
        else:
            q, kc, ks2, kw2, vc, vs2, vw2, gates = _proj_b(xbf, _prep_b_w_in(b_w_in[j]), tables, seq)
            pk, w1k, w2k = _prep_compress(b_cmp_pos_k[j], b_cmp_k_w1[j], b_cmp_k_w2[j], True)
            pv, w1v, w2v = _prep_compress(b_cmp_pos_v[j], b_cmp_v_w1[j], b_cmp_v_w2[j], False)
            kcc, vcc = _compress(kc, vc, pk, pv, w1k, w1v, w2k, w2v, batch, seq)
            ovt = _overlap_t(seq // CMP_STRIDE, seq // SEL_BLOCK)
            o = _nsa_attention(q, gates, kcc, vcc, ks2, vs2, kw2, vw2, ovt,
                               _key_block_indicator(seq), batch, seq)
            xf, xbf, route = _outproj_ln(o, b_w_out[j].astype(BF16), xf, ln_gain[i, 0], ln_bias[i, 0], alpha,
                                         router_t=_split_router(moe_router[j]))
            xf, xbf = _moe(xf, xbf, route, moe_w_gate[j], moe_w_up[j], moe_w_down[j],
                           ln_gain[i, 1], ln_bias[i, 1], alpha)
    return xf.reshape(batch, seq, d)
```

```python
import functools
import math

import numpy as np
import jax
import jax.numpy as jnp
from jax import lax
from jax.experimental import pallas as pl
from jax.experimental.pallas import tpu as pltpu

F32 = jnp.float32
BF16 = jnp.bfloat16

HEAD_DIM = 64
HALF_DIM = HEAD_DIM // 2
LANES = 128
MXU_WIDTH = 256
ROPE_THETA = 10000.0
LOG2E = math.log2(math.e)
Q_SCALE = HEAD_DIM ** -0.5 * LOG2E
KV_HEADS = 4
GROUP = 4
SWA_BLOCK = 128
SWA_WINDOW = 128
CMP_LEN = 32
CMP_STRIDE = 16
SEL_BLOCK = 64
SEL_SHIFT = 6
SEL_TOPN = 16
NSA_WINDOW = 512
SEL_FORCE = 1e4
N_GATES = 3
LN_EPS = 1e-5
NEG_INF = -1e30
SEL_MASK = 2.0 ** 100
VMEM_LIMIT = 48 * 1024 * 1024

NSA_TQ = 256
NSA_KC = 512


def _cparams(*sem):
    return pltpu.CompilerParams(dimension_semantics=sem, vmem_limit_bytes=VMEM_LIMIT)


def _dot(a, b):
    return jnp.dot(a, b, preferred_element_type=F32)


def _dot_nt(a, b):
    return lax.dot_general(a, b, (((1,), (1,)), ((), ())), preferred_element_type=F32)


def _dot_tn(a, b):
    return lax.dot_general(a, b, (((0,), (0,)), ((), ())), preferred_element_type=F32)


def _pair_cols(n_heads):
    idx = []
    for j in range(n_heads // 2):
        a, b = 2 * j * HEAD_DIM, (2 * j + 1) * HEAD_DIM
        idx += list(range(a, a + HALF_DIM)) + list(range(b, b + HALF_DIM))
        idx += list(range(a + HALF_DIM, a + HEAD_DIM)) + list(range(b + HALF_DIM, b + HEAD_DIM))
    return np.asarray(idx, np.int32)


def _kdup_cols(n_heads):
    idx = []
    for h in range(n_heads):
        a = h * HEAD_DIM
        idx += list(range(a, a + HALF_DIM)) * 2 + list(range(a + HALF_DIM, a + HEAD_DIM)) * 2
    return np.asarray(idx, np.int32)


def _vdup_cols(n_heads):
    idx = []
    for h in range(n_heads):
        idx += list(range(h * HEAD_DIM, (h + 1) * HEAD_DIM)) * 2
    return np.asarray(idx, np.int32)


def _rope_tables(seq):
    inv = 1.0 / (ROPE_THETA ** (jnp.arange(0, HEAD_DIM, 2, dtype=F32) / HEAD_DIM))
    ang = jnp.arange(seq, dtype=F32)[:, None] * inv[None, :]
    cos, sin = jnp.cos(ang), jnp.sin(ang)
    c, s = jnp.tile(cos, (1, 4)), jnp.concatenate([-sin, -sin, sin, sin], axis=1)
    return jnp.concatenate([c * Q_SCALE, s * Q_SCALE, c, s], axis=1)


def _rope(y, c, s):
    return y * c + pltpu.roll(y, LANES // 2, 1) * s


def _project_rope(x, w_ref, col, out_ref, c, s):
    width = out_ref.shape[1]
    step = min(MXU_WIDTH, width)
    for j in range(width // step):
        y = _dot(x, w_ref[:, col + j * step:col + (j + 1) * step])
        for i in range(step // LANES):
            lo = j * step + i * LANES
            out_ref[:, lo:lo + LANES] = _rope(y[:, i * LANES:(i + 1) * LANES], c, s).astype(out_ref.dtype)


def _split_tables(tab_ref):
    tab = tab_ref[...]
    return [tab[:, i * LANES:(i + 1) * LANES] for i in range(4)]


def _layer_norm(z, g, b):
    mu = jnp.mean(z, axis=-1, keepdims=True)
    zc = z - mu
    var = jnp.mean(zc * zc, axis=-1, keepdims=True)
    return zc * lax.rsqrt(var + LN_EPS) * g + b


def _proj_a_kernel(x_ref, w_ref, tab_ref, q_ref, k_ref, v_ref):
    x = x_ref[...].astype(BF16)
    cq, sq, c, s = _split_tables(tab_ref)
    nq = q_ref.shape[1]
    nk = k_ref.shape[1]
    _project_rope(x, w_ref, 0, q_ref, cq, sq)
    _project_rope(x, w_ref, nq, k_ref, c, s)
    v_ref[...] = _dot(x, w_ref[:, nq + nk:]).astype(BF16)


def _proj_a(x2d, w, tables, seq, tm=512):
    t, d = x2d.shape
    nq = d
    nk = KV_HEADS * LANES
    per_seq = seq // tm
    return pl.pallas_call(
        _proj_a_kernel,
        grid=(t // tm,),
        in_specs=[
            pl.BlockSpec((tm, d), lambda i: (i, 0)),
            pl.BlockSpec(w.shape, lambda i: (0, 0)),
            pl.BlockSpec((tm, 4 * LANES), lambda i: (i % per_seq, 0)),
        ],
        out_specs=[
            pl.BlockSpec((tm, nq), lambda i: (i, 0)),
            pl.BlockSpec((tm, nk), lambda i: (i, 0)),
            pl.BlockSpec((tm, nk), lambda i: (i, 0)),
        ],
        out_shape=[
            jax.ShapeDtypeStruct((t, nq), BF16),
            jax.ShapeDtypeStruct((t, nk), BF16),
            jax.ShapeDtypeStruct((t, nk), BF16),
        ],
        compiler_params=_cparams("parallel"),
        name="proj_a",
    )(x2d, w, tables)


def _proj_b_kernel(x_ref, w_ref, tab_ref,
                   q_ref, kc_ref, ks_ref, kw_ref, vc_ref, vs_ref, vw_ref, g_ref):
    x = x_ref[...].astype(BF16)
    cq, sq, c, s = _split_tables(tab_ref)
    col = 0
    for ref in (q_ref, kc_ref, ks_ref, kw_ref):
        cr, sr = (cq, sq) if ref is q_ref else (c, s)
        _project_rope(x, w_ref, col, ref, cr, sr)
        col += ref.shape[1]
    for ref in (vc_ref, vs_ref, vw_ref):
        n = ref.shape[1]
        ref[...] = _dot(x, w_ref[:, col:col + n]).astype(ref.dtype)
        col += n
    g_ref[...] = jax.nn.sigmoid(_dot(x, w_ref[:, col:]))


def _proj_b(x2d, w, tables, seq, tm=512):
    t, d = x2d.shape
    nkv = KV_HEADS * HEAD_DIM
    ndup = KV_HEADS * LANES
    widths = [(d, BF16), (nkv, F32), (ndup, BF16), (ndup, BF16),
              (nkv, F32), (ndup, BF16), (ndup, BF16), (KV_HEADS * LANES, F32)]
    per_seq = seq // tm
    return pl.pallas_call(
        _proj_b_kernel,
        grid=(t // tm,),
        in_specs=[
            pl.BlockSpec((tm, d), lambda i: (i, 0)),
            pl.BlockSpec(w.shape, lambda i: (0, 0)),
            pl.BlockSpec((tm, 4 * LANES), lambda i: (i % per_seq, 0)),
        ],
        out_specs=[pl.BlockSpec((tm, n), lambda i: (i, 0)) for n, _ in widths],
        out_shape=[jax.ShapeDtypeStruct((t, n), dt) for n, dt in widths],
        compiler_params=_cparams("parallel"),
        name="proj_b",
    )(x2d, w, tables)


def _stack_heads(q_ref, rows):
    lane = lax.broadcasted_iota(jnp.int32, (rows, LANES), 1)
    first = (lane & (HEAD_DIM - 1)) < HALF_DIM
    zero = jnp.zeros((rows, LANES), BF16)
    parts = []
    for p in range(GROUP // 2):
        qp = q_ref[:, p * LANES:(p + 1) * LANES]
        parts += [jnp.where(first, qp, zero), jnp.where(first, zero, qp)]
    return jnp.concatenate(parts, axis=0)


def _unstack_heads(o, rows):
    lane = lax.broadcasted_iota(jnp.int32, (rows, LANES), 1)
    low = lane < HEAD_DIM
    return [jnp.where(low, o[2 * p], o[2 * p + 1]) for p in range(GROUP // 2)]


def _swa_kernel(sink_ref, q_ref, kp_ref, kc_ref, vp_ref, vc_ref, o_ref):
    n = pl.program_id(1)
    blk = SWA_BLOCK
    qi = lax.broadcasted_iota(jnp.int32, (blk, 2 * blk), 0)
    si = lax.broadcasted_iota(jnp.int32, (blk, 2 * blk), 1)
    diff = qi + blk - si
    ok = (diff >= 0) & (diff < SWA_WINDOW) & ((si >= blk) | (n > 0))
    for g in range(KV_HEADS):
        q4 = _stack_heads(q_ref.at[:, g * 2 * LANES:(g + 1) * 2 * LANES], blk)
        kk = jnp.concatenate([kp_ref[:, g * LANES:(g + 1) * LANES], kc_ref[:, g * LANES:(g + 1) * LANES]], axis=0)
        vv = jnp.concatenate([vp_ref[:, g * LANES:(g + 1) * LANES], vc_ref[:, g * LANES:(g + 1) * LANES]], axis=0)
        s = _dot_nt(q4, kk).reshape(GROUP, blk, 2 * blk)
        s = jnp.where(ok[None], s, NEG_INF)
        ps, rs = [], []
        for h in range(GROUP):
            sink = sink_ref[g * GROUP + h] * LOG2E
            m = jnp.maximum(jnp.max(s[h], axis=-1, keepdims=True), sink)
            p = jnp.exp2(s[h] - m)
            rs.append(1.0 / (jnp.sum(p, axis=-1, keepdims=True) + jnp.exp2(sink - m)))
            ps.append(p.astype(BF16))
        o = _dot(jnp.concatenate(ps, axis=0), vv).reshape(GROUP, blk, LANES)
        o = [o[h] * rs[h] for h in range(GROUP)]
        for p, blkout in enumerate(_unstack_heads(o, blk)):
            c0 = (g * 2 + p) * LANES
            o_ref[:, c0:c0 + LANES] = blkout.astype(BF16)


def _swa_attention(q, k2, v2, sinks, batch, seq):
    t, nq = q.shape
    nk = k2.shape[1]
    nb = seq // SWA_BLOCK
    cur = lambda b, n: (b * nb + n, 0)
    prev = lambda b, n: (b * nb + jnp.maximum(n - 1, 0), 0)
    return pl.pallas_call(
        _swa_kernel,
        grid=(batch, nb),
        in_specs=[
            pl.BlockSpec(memory_space=pltpu.SMEM),
            pl.BlockSpec((SWA_BLOCK, nq), cur),
            pl.BlockSpec((SWA_BLOCK, nk), prev),
            pl.BlockSpec((SWA_BLOCK, nk), cur),
            pl.BlockSpec((SWA_BLOCK, nk), prev),
            pl.BlockSpec((SWA_BLOCK, nk), cur),
        ],
        out_specs=pl.BlockSpec((SWA_BLOCK, nq), cur),
        out_shape=jax.ShapeDtypeStruct((t, nq), BF16),
        compiler_params=_cparams("parallel", "parallel"),
        name="swa_attention",
    )(sinks, q, k2, k2, v2, v2)


def _outproj_ln_kernel(alpha, o_ref, w_ref, x_ref, g_ref, b_ref, y_ref, ybf_ref):
    z = alpha * x_ref[...].astype(F32) + _dot(o_ref[...], w_ref[...])
    y = _layer_norm(z, g_ref[...], b_ref[...])
    y_ref[...] = y
    ybf_ref[...] = y.astype(BF16)


def _outproj_ln_router_kernel(alpha, o_ref, w_ref, x_ref, g_ref, b_ref, r_ref, y_ref, ybf_ref, route_ref):
    z = alpha * x_ref[...].astype(F32) + _dot(o_ref[...], w_ref[...])
    y = _layer_norm(z, g_ref[...], b_ref[...])
    y_ref[...] = y
    ybf_ref[...] = y.astype(BF16)
    ne = r_ref.shape[0] // 2
    y_hi = y.astype(BF16)
    y_lo = (y - y_hi.astype(F32)).astype(BF16)
    part = _dot_nt(r_ref[...], y_hi)
    logits = part[0:ne] + part[ne:2 * ne] + _dot_nt(r_ref[0:ne, :], y_lo)
    eid = lax.broadcasted_iota(jnp.int32, logits.shape, 0)
    m1 = jnp.max(logits, axis=0, keepdims=True)
    i1 = jnp.min(jnp.where(logits == m1, eid, ne), axis=0, keepdims=True)
    rest = jnp.where(eid == i1, -jnp.inf, logits)
    m2 = jnp.max(rest, axis=0, keepdims=True)
    i2 = jnp.min(jnp.where(rest == m2, eid, ne), axis=0, keepdims=True)
    e = jnp.exp(m2 - m1)
    w1 = 1.0 / (1.0 + e)
    w2 = e / (1.0 + e)
    route_ref[...] = jnp.concatenate(
        [i1.astype(F32), i2.astype(F32), w1, w2, jnp.zeros((4, logits.shape[1]), F32)], axis=0)


def _outproj_ln(o, w, x, gain, bias, alpha, router_t=None, tm=512):
    t, d = x.shape
    row = lambda i: (i, 0)
    const = lambda i: (0, 0)
    in_specs = [
        pl.BlockSpec((tm, o.shape[1]), row),
        pl.BlockSpec(w.shape, const),
        pl.BlockSpec((tm, d), row),
        pl.BlockSpec((1, d), const),
        pl.BlockSpec((1, d), const),
    ]
    out_specs = [pl.BlockSpec((tm, d), row), pl.BlockSpec((tm, d), row)]
    out_shape = [jax.ShapeDtypeStruct((t, d), F32), jax.ShapeDtypeStruct((t, d), BF16)]
    args = [o, w, x, gain.reshape(1, d), bias.reshape(1, d)]
    if router_t is None:
        body = functools.partial(_outproj_ln_kernel, alpha)
        name = "outproj_ln"
    else:
        body = functools.partial(_outproj_ln_router_kernel, alpha)
        name = "outproj_ln_router"
        in_specs.append(pl.BlockSpec(router_t.shape, const))
        args.append(router_t)
        out_specs.append(pl.BlockSpec((8, tm), lambda i: (0, i)))
        out_shape.append(jax.ShapeDtypeStruct((8, t), F32))
    return pl.pallas_call(
        body,
        grid=(t // tm,),
        in_specs=in_specs,
        out_specs=out_specs,
        out_shape=out_shape,
        compiler_params=_cparams("parallel"),
        name=name,
    )(*args)


def _ffn_ln_kernel(alpha, xbf_ref, x_ref, wg_ref, wu_ref, wd_ref, g_ref, b_ref, y_ref, ybf_ref, acc_ref):
    j = pl.program_id(1)

    @pl.when(j == 0)
    def _():
        acc_ref[...] = jnp.zeros_like(acc_ref)

    xb = xbf_ref[...]
    h = jax.nn.silu(_dot(xb, wg_ref[...])) * _dot(xb, wu_ref[...])
    acc_ref[...] += _dot(h.astype(BF16), wd_ref[...])

    @pl.when(j == pl.num_programs(1) - 1)
    def _():
        y = _layer_norm(alpha * x_ref[...] + acc_ref[...], g_ref[...], b_ref[...])
        y_ref[...] = y
        ybf_ref[...] = y.astype(BF16)


def _ffn_ln(xbf, x, wg, wu, wd, gain, bias, alpha, tm=1024, tf=512):
    t, d = x.shape
    dff = wg.shape[1]
    row = lambda i, j: (i, 0)
    const = lambda i, j: (0, 0)
    return pl.pallas_call(
        functools.partial(_ffn_ln_kernel, alpha),
        grid=(t // tm, dff // tf),
        in_specs=[
            pl.BlockSpec((tm, d), row),
            pl.BlockSpec((tm, d), row),
            pl.BlockSpec((d, tf), lambda i, j: (0, j)),
            pl.BlockSpec((d, tf), lambda i, j: (0, j)),
            pl.BlockSpec((tf, d), lambda i, j: (j, 0)),
            pl.BlockSpec((1, d), const),
            pl.BlockSpec((1, d), const),
        ],
        out_specs=[pl.BlockSpec((tm, d), row), pl.BlockSpec((tm, d), row)],
        out_shape=[jax.ShapeDtypeStruct((t, d), F32), jax.ShapeDtypeStruct((t, d), BF16)],
        scratch_shapes=[pltpu.VMEM((tm, d), F32)],
        compiler_params=_cparams("parallel", "arbitrary"),
        name="ffn_ln",
    )(xbf, x, wg, wu, wd, gain.reshape(1, d), bias.reshape(1, d))


def _gelu_tanh(x):
    return 0.5 * x * (1.0 + jnp.tanh(math.sqrt(2.0 / math.pi) * (x + 0.044715 * (x * x * x))))


def _compress_kernel(kc_ref, vc_ref, pk_ref, pv_ref, w1k_ref, w1v_ref, w2k_ref, w2v_ref, ko_ref, vo_ref):
    nseg = ko_ref.shape[0]
    half = CMP_LEN // 2

    def one(t_ref, pos_ref, w1_ref, w2_ref, out_ref):
        hidden = w1_ref.shape[2]
        lo = jnp.zeros((nseg, hidden), F32)
        hi = jnp.zeros((nseg, hidden), F32)
        for l in range(half):
            rows = t_ref[pl.ds(l, nseg, stride=CMP_STRIDE), :]
            lo += _dot((rows + pos_ref[l:l + 1, :]).astype(BF16), w1_ref[l])
            hi += _dot((rows + pos_ref[half + l:half + l + 1, :]).astype(BF16), w1_ref[half + l])
        h = _gelu_tanh(lo + pltpu.roll(hi, nseg - 1, 0))
        out_ref[...] = _dot(h.astype(BF16), w2_ref[...]).astype(out_ref.dtype)

    one(kc_ref, pk_ref, w1k_ref, w2k_ref, ko_ref)
    one(vc_ref, pv_ref, w1v_ref, w2v_ref, vo_ref)


def _compress(kc, vc, pk, pv, w1k, w1v, w2k, w2v, batch, seq):
    nseg = seq // CMP_STRIDE
    npair = KV_HEADS // 2
    tok = lambda b, p: (b, p)
    c2 = lambda b, p: (0, 0)
    c3 = lambda b, p: (0, 0, 0)
    return pl.pallas_call(
        _compress_kernel,
        grid=(batch, npair),
        in_specs=[
            pl.BlockSpec((seq, LANES), tok),
            pl.BlockSpec((seq, LANES), tok),
            pl.BlockSpec(pk.shape, c2),
            pl.BlockSpec(pv.shape, c2),
            pl.BlockSpec(w1k.shape, c3),
            pl.BlockSpec(w1v.shape, c3),
            pl.BlockSpec(w2k.shape, c2),
            pl.BlockSpec(w2v.shape, c2),
        ],
        out_specs=[pl.BlockSpec((nseg, 2 * LANES), tok), pl.BlockSpec((nseg, 2 * LANES), tok)],
        out_shape=[jax.ShapeDtypeStruct((batch * nseg, KV_HEADS * LANES), BF16)] * 2,
        compiler_params=_cparams("parallel", "parallel"),
        name="nsa_compress",
    )(kc, vc, pk, pv, w1k, w1v, w2k, w2v)


def _lane_fold(x, op):
    out = x[:, 0:LANES]
    for j in range(1, x.shape[1] // LANES):
        out = op(out, x[:, j * LANES:(j + 1) * LANES])
    return out


def _nsa_kernel(q_ref, gate_ref, kcc_ref, vcc_ref, ks_ref, vs_ref, kw_ref, vw_ref, ovt_ref, blk_ref,
                o_ref, qa_ref, imp_ref, s_ref, mx_ref, l_ref, acc_ref, part_ref):
    tq, kc = NSA_TQ, NSA_KC
    rows = GROUP * tq
    ncp = kcc_ref.shape[0]
    nsel = ovt_ref.shape[0]
    topn = min(SEL_TOPN, nsel)
    t0 = pl.program_id(2) * tq
    qa_ref[:, 0:LANES] = _stack_heads(q_ref, tq)
    tpos = t0 + lax.broadcasted_iota(jnp.int32, (tq, 1), 0)
    gates = gate_ref[...]

    def gate_column(branch):
        return jnp.concatenate([gates[:, N_GATES * h + branch:N_GATES * h + branch + 1] for h in range(GROUP)],
                               axis=0)

    cend = CMP_STRIDE * lax.broadcasted_iota(jnp.int32, (1, ncp), 1) + (CMP_LEN - 1)
    ok_c = (cend <= tpos)[None]
    s = jnp.where(ok_c, _dot_nt(qa_ref[:, 0:LANES], kcc_ref[...]).reshape(GROUP, tq, ncp), NEG_INF)
    p = jnp.where(ok_c, jnp.exp2(s - jnp.max(s, axis=-1, keepdims=True)), 0.0)
    p = p * (1.0 / jnp.maximum(jnp.sum(p, axis=-1, keepdims=True), 1e-30))
    o_c = _dot(p.reshape(rows, ncp).astype(BF16), vcc_ref[...])
    psum = p[0] + p[1] + p[2] + p[3]

    span = NSA_WINDOW + tq
    w0 = pl.multiple_of(jnp.maximum(t0 - NSA_WINDOW, 0), LANES)
    d = tpos - (w0 + lax.broadcasted_iota(jnp.int32, (1, span), 1))
    bias = jnp.where((d >= 0) & (d < NSA_WINDOW), 0.0, NEG_INF)
    s = _dot_nt(qa_ref[:, 0:LANES], kw_ref[pl.ds(w0, span), :])
    s = (s.reshape(GROUP, tq, span) + bias[None]).reshape(rows, span)
    mw = jnp.broadcast_to(jnp.max(_lane_fold(s, jnp.maximum), axis=-1, keepdims=True), (rows, LANES))
    p = [jnp.exp2(s[:, j * LANES:(j + 1) * LANES] - mw) for j in range(span // LANES)]
    lw = jnp.sum(functools.reduce(jnp.add, p), axis=-1, keepdims=True)
    o_w = _dot(jnp.concatenate([x.astype(BF16) for x in p], axis=1), vw_ref[pl.ds(w0, span), :])
    part_ref[...] = gate_column(0) * o_c + (gate_column(2) / lw) * o_w

    ovt = ovt_ref[...]
    imp = jnp.zeros((nsel, tq), F32)
    rem = psum
    for _ in range(3):
        part = rem.astype(BF16)
        imp += _dot_nt(ovt, part)
        rem = rem - part.astype(F32)
    blk = lax.broadcasted_iota(jnp.int32, (nsel, tq), 0)
    cur = jnp.right_shift(t0 + lax.broadcasted_iota(jnp.int32, (nsel, tq), 1), SEL_SHIFT)
    forced = (blk == 0) | (blk == cur) | (blk == cur - 1)
    imp = jnp.where(forced, SEL_FORCE, imp)
    imp = jnp.where(blk <= cur, imp, -1.0)
    imp_ref[...] = imp

    ngrp = nsel // 8
    sub = lax.broadcasted_iota(jnp.int32, (8, tq), 0)
    grp = [imp[8 * r:8 * r + 8] for r in range(ngrp)]
    cnt = [jnp.zeros((8, tq), F32) for _ in range(ngrp)]
    for i in range(nsel):
        row = jnp.broadcast_to(imp_ref[i:i + 1, :], (8, tq))
        for r in range(ngrp):
            ge = jnp.where(row >= grp[r], 1.0, 0.0)
            gt = jnp.where(row > grp[r], 1.0, 0.0)
            if 8 * r > i:
                cnt[r] = cnt[r] + ge
            elif 8 * r + 7 < i:
                cnt[r] = cnt[r] + gt
            else:
                cnt[r] = cnt[r] + jnp.where(sub > i - 8 * r, ge, gt)
    rank = jnp.concatenate(cnt, axis=0)
    unsel = jnp.where((rank < topn) & (blk <= cur), 0.0, -1.0)
    if nsel < LANES:
        unsel = jnp.concatenate([unsel, jnp.zeros((LANES - nsel, tq), F32)], axis=0)
    selq = unsel.T.astype(BF16)
    for h in range(GROUP):
        qa_ref[h * tq:(h + 1) * tq, LANES:2 * LANES] = selq

    def key_side(start):
        return jnp.concatenate([ks_ref[pl.ds(start, kc), :], blk_ref[pl.ds(start, kc), :]], axis=1)

    nfull = t0 // kc
    dstart = pl.multiple_of(nfull * kc, kc)
    causal = jnp.where(dstart + lax.broadcasted_iota(jnp.int32, (1, kc), 1) <= tpos, 0.0, NEG_INF)
    s = (_dot_nt(qa_ref[...], key_side(dstart)).reshape(GROUP, tq, kc) + causal[None]).reshape(rows, kc)
    s_ref[nfull] = s
    mx_ref[...] = _lane_fold(s, jnp.maximum)

    def score_chunks(first, count):
        mx = mx_ref[...]
        for u in range(count):
            s = _dot_nt(qa_ref[...], key_side(pl.multiple_of((first + u) * kc, kc)))
            s_ref[first + u] = s
            mx = jnp.maximum(mx, _lane_fold(s, jnp.maximum))
        mx_ref[...] = mx

    def score_pair(c, carry):
        score_chunks(2 * c, 2)
        return carry

    lax.fori_loop(0, nfull // 2, score_pair, 0)

    @pl.when(nfull % 2 == 1)
    def _():
        score_chunks(nfull - 1, 1)

    mx_ref[...] = jnp.broadcast_to(jnp.max(mx_ref[...], axis=-1, keepdims=True), (rows, LANES))
    l_ref[...] = jnp.zeros(l_ref.shape, F32)
    acc_ref[...] = jnp.zeros(acc_ref.shape, F32)

    def value_chunks(first, count):
        mrep = mx_ref[...]
        lsum = l_ref[...]
        acc = acc_ref[...]
        for u in range(count):
            sc = s_ref[first + u]
            p = [jnp.exp2(sc[:, j * LANES:(j + 1) * LANES] - mrep) for j in range(kc // LANES)]
            lsum = lsum + functools.reduce(jnp.add, p)
            pb = jnp.concatenate([x.astype(BF16) for x in p], axis=1)
            acc = acc + _dot(pb, vs_ref[pl.ds(pl.multiple_of((first + u) * kc, kc), kc), :])
        l_ref[...] = lsum
        acc_ref[...] = acc

    def value_pair(c, carry):
        value_chunks(2 * c, 2)
        return carry

    lax.fori_loop(0, (nfull + 1) // 2, value_pair, 0)

    @pl.when(nfull % 2 == 0)
    def _():
        value_chunks(nfull, 1)

    ls = jnp.sum(l_ref[...], axis=-1, keepdims=True)
    out = part_ref[...] + (gate_column(1) / ls) * acc_ref[...]
    for pidx, blkout in enumerate(_unstack_heads(out.reshape(GROUP, tq, LANES), tq)):
        o_ref[:, pidx * LANES:(pidx + 1) * LANES] = blkout.astype(BF16)


def _nsa_attention(q, gates, kcc, vcc, ks2, vs2, kw2, vw2, ovt, key_blk, batch, seq):
    t, nq = q.shape
    tq = NSA_TQ
    rows = GROUP * tq
    nt = seq // tq
    ncp = seq // CMP_STRIDE
    nsel = seq // SEL_BLOCK
    qmap = lambda b, g, i: (b * nt + i, g)
    kvmap = lambda b, g, i: (b, g)
    return pl.pallas_call(
        _nsa_kernel,
        grid=(batch, KV_HEADS, nt),
        in_specs=[
            pl.BlockSpec((tq, 2 * LANES), qmap),
            pl.BlockSpec((tq, LANES), qmap),
            pl.BlockSpec((ncp, LANES), kvmap),
            pl.BlockSpec((ncp, LANES), kvmap),
            pl.BlockSpec((seq, LANES), kvmap),
            pl.BlockSpec((seq, LANES), kvmap),
            pl.BlockSpec((seq, LANES), kvmap),
            pl.BlockSpec((seq, LANES), kvmap),
            pl.BlockSpec(ovt.shape, lambda b, g, i: (0, 0)),
            pl.BlockSpec(key_blk.shape, lambda b, g, i: (0, 0)),
        ],
        out_specs=pl.BlockSpec((tq, 2 * LANES), qmap),
        out_shape=jax.ShapeDtypeStruct((t, nq), BF16),
        scratch_shapes=[
            pltpu.VMEM((rows, 2 * LANES), BF16),
            pltpu.VMEM((nsel, tq), F32),
            pltpu.VMEM((seq // NSA_KC, rows, NSA_KC), F32),
            pltpu.VMEM((rows, LANES), F32),
            pltpu.VMEM((rows, LANES), F32),
            pltpu.VMEM((rows, LANES), F32),
            pltpu.VMEM((rows, LANES), F32),
        ],
        compiler_params=_cparams("parallel", "parallel", "arbitrary"),
        name="nsa_attention",
    )(q, gates, kcc, vcc, ks2, vs2, kw2, vw2, ovt, key_blk)


def _moe_kernel(te_ref, nu_ref, xs_ref, wg_ref, wu_ref, wd_ref, ys_ref, acc_ref):
    i = pl.program_id(0)
    j = pl.program_id(1)

    @pl.when(i < nu_ref[0])
    def _():
        @pl.when(j == 0)
        def _():
            acc_ref[...] = jnp.zeros_like(acc_ref)

        xb = xs_ref[...]
        h = jax.nn.silu(_dot(xb, wg_ref[0].astype(BF16))) * _dot(xb, wu_ref[0].astype(BF16))
        acc_ref[...] += _dot(h.astype(BF16), wd_ref[0].astype(BF16))

        @pl.when(j == pl.num_programs(1) - 1)
        def _():
            ys_ref[...] = acc_ref[...].astype(ys_ref.dtype)

    @pl.when((i >= nu_ref[0]) & (j == pl.num_programs(1) - 1))
    def _():
        ys_ref[...] = jnp.zeros_like(ys_ref)


def _moe_ffn(xs, wg, wu, wd, tile_expert, n_used, tm, tf=512):
    p, d = xs.shape
    dff = wg.shape[2]
    nj = dff // tf

    def rows(i, j, te, nu):
        return (jnp.minimum(i, nu[0] - 1), 0)

    def jj(i, j, nu):
        return jnp.where(i < nu[0], j, nj - 1)

    return pl.pallas_call(
        _moe_kernel,
        grid_spec=pltpu.PrefetchScalarGridSpec(
            num_scalar_prefetch=2,
            grid=(p // tm, nj),
            in_specs=[
                pl.BlockSpec((tm, d), rows),
                pl.BlockSpec((1, d, tf), lambda i, j, te, nu: (te[i], 0, jj(i, j, nu))),
                pl.BlockSpec((1, d, tf), lambda i, j, te, nu: (te[i], 0, jj(i, j, nu))),
                pl.BlockSpec((1, tf, d), lambda i, j, te, nu: (te[i], jj(i, j, nu), 0)),
            ],
            out_specs=pl.BlockSpec((tm, d), lambda i, j, te, nu: (i, 0)),
            scratch_shapes=[pltpu.VMEM((tm, d), F32)],
        ),
        out_shape=jax.ShapeDtypeStruct((p, d), BF16),
        compiler_params=_cparams("arbitrary", "arbitrary"),
        name="moe_ffn",
    )(tile_expert, n_used, xs, wg, wu, wd)


def _combine_ln_kernel(alpha, x_ref, y1_ref, y2_ref, w_ref, g_ref, b_ref, o_ref, obf_ref):
    w = w_ref[...]
    f = w[:, 0:1] * y1_ref[...].astype(F32) + w[:, 1:2] * y2_ref[...].astype(F32)
    y = _layer_norm(alpha * x_ref[...] + f, g_ref[...], b_ref[...])
    o_ref[...] = y
    obf_ref[...] = y.astype(BF16)


def _combine_ln(x, y1, y2, wcol, gain, bias, alpha, tm=512):
    t, d = x.shape
    row = lambda i: (i, 0)
    const = lambda i: (0, 0)
    return pl.pallas_call(
        functools.partial(_combine_ln_kernel, alpha),
        grid=(t // tm,),
        in_specs=[
            pl.BlockSpec((tm, d), row),
            pl.BlockSpec((tm, d), row),
            pl.BlockSpec((tm, d), row),
            pl.BlockSpec((tm, 2), row),
            pl.BlockSpec((1, d), const),
            pl.BlockSpec((1, d), const),
        ],
        out_specs=[pl.BlockSpec((tm, d), row), pl.BlockSpec((tm, d), row)],
        out_shape=[jax.ShapeDtypeStruct((t, d), F32), jax.ShapeDtypeStruct((t, d), BF16)],
        compiler_params=_cparams("parallel"),
        name="combine_ln",
    )(x, y1, y2, wcol, gain.reshape(1, d), bias.reshape(1, d))


def _moe(x, xbf, route, wg, wu, wd, gain, bias, alpha, tm=1024):
    t, d = x.shape
    ne = wg.shape[0]
    ids = route[0:2].astype(jnp.int32)
    experts = jnp.arange(ne, dtype=jnp.int32)[None, :]
    hit1 = ids[0][:, None] == experts
    hit2 = ids[1][:, None] == experts
    csum = jnp.cumsum((hit1 | hit2).astype(jnp.int32), axis=0)
    counts = csum[-1]
    padded = ((counts + tm - 1) // tm) * tm
    ends = jnp.cumsum(padded)
    starts = ends - padded
    slot = starts[None, :] + csum - 1
    pos = jnp.concatenate([jnp.sum(jnp.where(hit1, slot, 0), axis=1), jnp.sum(jnp.where(hit2, slot, 0), axis=1)])
    n_rows = 2 * t + ne * tm
    n_tiles = n_rows // tm
    tile_start = jnp.arange(n_tiles, dtype=jnp.int32) * tm
    tile_expert = jnp.minimum(jnp.sum((tile_start[:, None] >= ends[None, :]).astype(jnp.int32), axis=1), ne - 1)
    n_used = (ends[-1] // tm).astype(jnp.int32)
    tok = jnp.tile(jnp.arange(t, dtype=jnp.int32), 2)
    _, tok_sorted = lax.sort_key_val(pos, tok)
    tok_sorted = jnp.concatenate([tok_sorted, jnp.arange(tm, dtype=jnp.int32)])
    first = (jnp.cumsum(counts) - counts)[tile_expert] + tile_start - starts[tile_expert]
    first = jnp.clip(first, 0, 2 * t)
    rows_of_tile = first[:, None] + jnp.arange(tm, dtype=jnp.int32)[None, :]
    src = jnp.take(tok_sorted, rows_of_tile.reshape(n_rows), mode="clip")
    last_expert = tile_expert[jnp.maximum(n_used - 1, 0)]
    tile_expert = jnp.where(jnp.arange(n_tiles) < n_used, tile_expert, last_expert).astype(jnp.int32)
    xs = jnp.take(xbf, src, axis=0, mode="clip")
    ys = _moe_ffn(xs, wg, wu, wd, tile_expert, n_used.reshape(1), tm)
    y1 = jnp.take(ys, pos[:t], axis=0, mode="clip")
    y2 = jnp.take(ys, pos[t:], axis=0, mode="clip")
    wcol = route[2:4].T
    return _combine_ln(x, y1, y2, wcol, gain, bias, alpha)


def _prep_a_w_in(w):
    d = w.shape[0]
    nq = d
    nkv = KV_HEADS * HEAD_DIM
    q = w[:, :nq][:, _pair_cols(nq // HEAD_DIM)]
    k = w[:, nq:nq + nkv][:, _kdup_cols(KV_HEADS)]
    v = w[:, nq + nkv:][:, _vdup_cols(KV_HEADS)]
    return jnp.concatenate([q, k, v], axis=1).astype(BF16)


def _prep_b_w_in(w):
    d = w.shape[0]
    nq = d
    nkv = KV_HEADS * HEAD_DIM
    part = lambda i: w[:, nq + i * nkv:nq + (i + 1) * nkv]
    q = w[:, :nq][:, _pair_cols(nq // HEAD_DIM)]
    kc = part(0)[:, _pair_cols(KV_HEADS)]
    vc = part(1)
    ks = part(2)[:, _kdup_cols(KV_HEADS)]
    vs = part(3)[:, _vdup_cols(KV_HEADS)]
    kw = part(4)[:, _kdup_cols(KV_HEADS)]
    vw = part(5)[:, _vdup_cols(KV_HEADS)]
    gl = w[:, nq + 6 * nkv:]
    per = GROUP * N_GATES
    gcols = []
    for g in range(KV_HEADS):
        gcols.append(jnp.pad(gl[:, g * per:(g + 1) * per], ((0, 0), (0, LANES - per))))
    return jnp.concatenate([q, kc, ks, kw, vc, vs, vw] + gcols, axis=1).astype(BF16)


def _prep_compress(pos, w1, w2, rope_layout):
    hidden = w1.shape[1]
    w1 = w1.reshape(CMP_LEN, HEAD_DIM, hidden)
    zeros = jnp.zeros_like(w1)
    a = jnp.concatenate([w1, zeros], axis=2)
    b = jnp.concatenate([zeros, w1], axis=2)
    natural = jnp.concatenate([a, b], axis=1)
    pos2 = jnp.concatenate([pos, pos], axis=1)
    if rope_layout:
        cols = _pair_cols(2)
        natural, pos2 = natural[:, cols, :], pos2[:, cols]
        out_cols = _kdup_cols(1)
    else:
        out_cols = _vdup_cols(1)
    w2d = w2[:, out_cols]
    z2 = jnp.zeros_like(w2d)
    w2p = jnp.concatenate([jnp.concatenate([w2d, z2], axis=1), jnp.concatenate([z2, w2d], axis=1)], axis=0)
    return pos2.astype(F32), natural.astype(BF16), w2p.astype(BF16)


def _split_router(router):
    rt = router.T
    hi = rt.astype(BF16)
    lo = (rt - hi.astype(F32)).astype(BF16)
    return jnp.concatenate([hi, lo], axis=0)


def _overlap_t(ncp, nsel):
    cs = CMP_STRIDE * np.arange(ncp)
    ce = cs + CMP_LEN
    ss = SEL_BLOCK * np.arange(nsel)
    se = ss + SEL_BLOCK
    ov = np.clip(np.minimum(ce[None, :], se[:, None]) - np.maximum(cs[None, :], ss[:, None]), 0, None)
    return jnp.asarray(ov / CMP_STRIDE, dtype=BF16)


def _key_block_indicator(seq):
    assert seq // SEL_BLOCK <= LANES
    hit = (np.arange(seq) // SEL_BLOCK)[:, None] == np.arange(LANES)[None, :]
    return jnp.asarray(np.where(hit, SEL_MASK, 0.0), dtype=BF16)


def kernel(x, a_w_in, a_w_out, a_sinks, b_w_in, b_w_out, b_cmp_pos_k, b_cmp_pos_v, b_cmp_k_w1, b_cmp_k_w2, b_cmp_v_w1, b_cmp_v_w2, ffn_w_gate, ffn_w_up, ffn_w_down, moe_router, moe_w_gate, moe_w_up, moe_w_down, ln_gain, ln_bias):
    batch, seq, d = x.shape
    depth = ln_gain.shape[0]
    alpha = float((2 * depth) ** 0.25)
    tables = _rope_tables(seq)
    xf = x.reshape(batch * seq, d)
    xbf = xf
    for i in range(depth):
        j = i // 2
        if i % 2 == 0:
            q, k2, v2 = _proj_a(xbf, _prep_a_w_in(a_w_in[j]), tables, seq)
            o = _swa_attention(q, k2, v2, a_sinks[j], batch, seq)
            xf, xbf = _outproj_ln(o, a_w_out[j].astype(BF16), xf, ln_gain[i, 0], ln_bias[i, 0], alpha)
            xf, xbf = _ffn_ln(xbf, xf, ffn_w_gate[j].astype(BF16), ffn_w_up[j].astype(BF16),
                              ffn_w_down[j].astype(BF16), ln_gain[i, 1], ln_bias[i, 1], alpha)
        else:
            q, kc, ks2, kw2, vc, vs2, vw2, gates = _proj_b(xbf, _prep_b_w_in(b_w_in[j]), tables, seq)
            pk, w1k, w2k = _prep_compress(b_cmp_pos_k[j], b_cmp_k_w1[j], b_cmp_k_w2[j], True)
            pv, w1v, w2v = _prep_compress(b_cmp_pos_v[j], b_cmp_v_w1[j], b_cmp_v_w2[j], False)
            kcc, vcc = _compress(kc, vc, pk, pv, w1k, w1v, w2k, w2v, batch, seq)
            ovt = _overlap_t(seq // CMP_STRIDE, seq // SEL_BLOCK)
            o = _nsa_attention(q, gates, kcc, vcc, ks2, vs2, kw2, vw2, ovt,
                               _key_block_indicator(seq), batch, seq)
            xf, xbf, route = _outproj_ln(o, b_w_out[j].astype(BF16), xf, ln_gain[i, 0], ln_bias[i, 0], alpha,
                                         router_t=_split_router(moe_router[j]))
            xf, xbf = _moe(xf, xbf, route, moe_w_gate[j], moe_w_up[j], moe_w_down[j],
                           ln_gain[i, 1], ln_bias[i, 1], alpha)
    return xf.reshape(batch, seq, d)
```

```python
import functools
import math

import numpy as np
import jax
import jax.numpy as jnp
from jax import lax
from jax.experimental import pallas as pl
from jax.experimental.pallas import tpu as pltpu

F32 = jnp.float32
BF16 = jnp.bfloat16

HEAD_DIM = 64
HALF_DIM = HEAD_DIM // 2
LANES = 128
MXU_WIDTH = 256
ROPE_THETA = 10000.0
LOG2E = math.log2(math.e)
Q_SCALE = HEAD_DIM ** -0.5 * LOG2E
KV_HEADS = 4
GROUP = 4
SWA_BLOCK = 128
SWA_WINDOW = 128
CMP_LEN = 32
CMP_STRIDE = 16
SEL_BLOCK = 64
SEL_SHIFT = 6
SEL_TOPN = 16
NSA_WINDOW = 512
SEL_FORCE = 1e4
N_GATES = 3
LN_EPS = 1e-5
NEG_INF = -1e30
SEL_MASK = 2.0 ** 100
VMEM_LIMIT = 48 * 1024 * 1024

NSA_TQ = 256
NSA_KC = 512


def _cparams(*sem):
    return pltpu.CompilerParams(dimension_semantics=sem, vmem_limit_bytes=VMEM_LIMIT)


def _dot(a, b):
    return jnp.dot(a, b, preferred_element_type=F32)


def _dot_nt(a, b):
    return lax.dot_general(a, b, (((1,), (1,)), ((), ())), preferred_element_type=F32)


def _dot_tn(a, b):
    return lax.dot_general(a, b, (((0,), (0,)), ((), ())), preferred_element_type=F32)


def _pair_cols(n_heads):
    idx = []
    for j in range(n_heads // 2):
        a, b = 2 * j * HEAD_DIM, (2 * j + 1) * HEAD_DIM
        idx += list(range(a, a + HALF_DIM)) + list(range(b, b + HALF_DIM))
        idx += list(range(a + HALF_DIM, a + HEAD_DIM)) + list(range(b + HALF_DIM, b + HEAD_DIM))
    return np.asarray(idx, np.int32)


def _kdup_cols(n_heads):
    idx = []
    for h in range(n_heads):
        a = h * HEAD_DIM
        idx += list(range(a, a + HALF_DIM)) * 2 + list(range(a + HALF_DIM, a + HEAD_DIM)) * 2
    return np.asarray(idx, np.int32)


def _vdup_cols(n_heads):
    idx = []
    for h in range(n_heads):
        idx += list(range(h * HEAD_DIM, (h + 1) * HEAD_DIM)) * 2
    return np.asarray(idx, np.int32)


def _rope_tables(seq):
    inv = 1.0 / (ROPE_THETA ** (jnp.arange(0, HEAD_DIM, 2, dtype=F32) / HEAD_DIM))
    ang = jnp.arange(seq, dtype=F32)[:, None] * inv[None, :]
    cos, sin = jnp.cos(ang), jnp.sin(ang)
    c, s = jnp.tile(cos, (1, 4)), jnp.concatenate([-sin, -sin, sin, sin], axis=1)
    return jnp.concatenate([c * Q_SCALE, s * Q_SCALE, c, s], axis=1)


def _rope(y, c, s):
    return y * c + pltpu.roll(y, LANES // 2, 1) * s


def _project_rope(x, w_ref, col, out_ref, c, s, spread=False):
    width = out_ref.shape[1] // 2 if spread else out_ref.shape[1]
    step = min(MXU_WIDTH, width)
    lane = lax.broadcasted_iota(jnp.int32, (x.shape[0], LANES), 1)
    first = (lane & (HEAD_DIM - 1)) < HALF_DIM
    for j in range(width // step):
        y = _dot(x, w_ref[:, col + j * step:col + (j + 1) * step])
        for i in range(step // LANES):
            blk = (j * step) // LANES + i
            r = _rope(y[:, i * LANES:(i + 1) * LANES], c, s)
            if spread:
                a = jnp.where(first, r, pltpu.roll(r, HALF_DIM, 1))
                b = jnp.where(first, pltpu.roll(r, LANES - HALF_DIM, 1), r)
                out_ref[:, 2 * blk * LANES:(2 * blk + 1) * LANES] = a.astype(out_ref.dtype)
                out_ref[:, (2 * blk + 1) * LANES:(2 * blk + 2) * LANES] = b.astype(out_ref.dtype)
            else:
                out_ref[:, blk * LANES:(blk + 1) * LANES] = r.astype(out_ref.dtype)


def _project_values(x, w_ref, col, out_ref):
    width = out_ref.shape[1] // 2
    y = _dot(x, w_ref[:, col:col + width])
    low = lax.broadcasted_iota(jnp.int32, (x.shape[0], LANES), 1) < HEAD_DIM
    for i in range(width // LANES):
        v = y[:, i * LANES:(i + 1) * LANES]
        swapped = pltpu.roll(v, HEAD_DIM, 1)
        out_ref[:, 2 * i * LANES:(2 * i + 1) * LANES] = jnp.where(low, v, swapped).astype(out_ref.dtype)
        out_ref[:, (2 * i + 1) * LANES:(2 * i + 2) * LANES] = jnp.where(low, swapped, v).astype(out_ref.dtype)


def _split_tables(tab_ref):
    tab = tab_ref[...]
    return [tab[:, i * LANES:(i + 1) * LANES] for i in range(4)]


def _layer_norm(z, g, b):
    mu = jnp.mean(z, axis=-1, keepdims=True)
    zc = z - mu
    var = jnp.mean(zc * zc, axis=-1, keepdims=True)
    return zc * lax.rsqrt(var + LN_EPS) * g + b


def _proj_a_kernel(x_ref, w_ref, tab_ref, q_ref, k_ref, v_ref):
    x = x_ref[...].astype(BF16)
    cq, sq, c, s = _split_tables(tab_ref)
    nq = q_ref.shape[1]
    nk = k_ref.shape[1]
    _project_rope(x, w_ref, 0, q_ref, cq, sq)
    _project_rope(x, w_ref, nq, k_ref, c, s, spread=True)
    _project_values(x, w_ref, nq + nk // 2, v_ref)


def _proj_a(x2d, w, tables, seq, tm=512):
    t, d = x2d.shape
    nq = d
    nk = KV_HEADS * LANES
    per_seq = seq // tm
    return pl.pallas_call(
        _proj_a_kernel,
        grid=(t // tm,),
        in_specs=[
            pl.BlockSpec((tm, d), lambda i: (i, 0)),
            pl.BlockSpec(w.shape, lambda i: (0, 0)),
            pl.BlockSpec((tm, 4 * LANES), lambda i: (i % per_seq, 0)),
        ],
        out_specs=[
            pl.BlockSpec((tm, nq), lambda i: (i, 0)),
            pl.BlockSpec((tm, nk), lambda i: (i, 0)),
            pl.BlockSpec((tm, nk), lambda i: (i, 0)),
        ],
        out_shape=[
            jax.ShapeDtypeStruct((t, nq), BF16),
            jax.ShapeDtypeStruct((t, nk), BF16),
            jax.ShapeDtypeStruct((t, nk), BF16),
        ],
        compiler_params=_cparams("parallel"),
        name="proj_a",
    )(x2d, w, tables)


def _proj_b_kernel(x_ref, w_ref, tab_ref,
                   q_ref, kc_ref, ks_ref, kw_ref, vc_ref, vs_ref, vw_ref, g_ref):
    x = x_ref[...].astype(BF16)
    cq, sq, c, s = _split_tables(tab_ref)
    col = 0
    _project_rope(x, w_ref, col, q_ref, cq, sq)
    col += q_ref.shape[1]
    _project_rope(x, w_ref, col, kc_ref, c, s)
    col += kc_ref.shape[1]
    for ref in (ks_ref, kw_ref):
        _project_rope(x, w_ref, col, ref, c, s, spread=True)
        col += ref.shape[1] // 2
    vc_ref[...] = _dot(x, w_ref[:, col:col + vc_ref.shape[1]])
    col += vc_ref.shape[1]
    for ref in (vs_ref, vw_ref):
        _project_values(x, w_ref, col, ref)
        col += ref.shape[1] // 2
    g_ref[...] = jax.nn.sigmoid(_dot(x, w_ref[:, col:]))


def _proj_b(x2d, w, tables, seq, tm=512):
    t, d = x2d.shape
    nkv = KV_HEADS * HEAD_DIM
    ndup = KV_HEADS * LANES
    widths = [(d, BF16), (nkv, F32), (ndup, BF16), (ndup, BF16),
              (nkv, F32), (ndup, BF16), (ndup, BF16), (KV_HEADS * LANES, F32)]
    per_seq = seq // tm
    return pl.pallas_call(
        _proj_b_kernel,
        grid=(t // tm,),
        in_specs=[
            pl.BlockSpec((tm, d), lambda i: (i, 0)),
            pl.BlockSpec(w.shape, lambda i: (0, 0)),
            pl.BlockSpec((tm, 4 * LANES), lambda i: (i % per_seq, 0)),
        ],
        out_specs=[pl.BlockSpec((tm, n), lambda i: (i, 0)) for n, _ in widths],
        out_shape=[jax.ShapeDtypeStruct((t, n), dt) for n, dt in widths],
        compiler_params=_cparams("parallel"),
        name="proj_b",
    )(x2d, w, tables)


def _stack_heads(q_ref, rows):
    lane = lax.broadcasted_iota(jnp.int32, (rows, LANES), 1)
    first = (lane & (HEAD_DIM - 1)) < HALF_DIM
    zero = jnp.zeros((rows, LANES), BF16)
    parts = []
    for p in range(GROUP // 2):
        qp = q_ref[:, p * LANES:(p + 1) * LANES]
        parts += [jnp.where(first, qp, zero), jnp.where(first, zero, qp)]
    return jnp.concatenate(parts, axis=0)


def _unstack_heads(o, rows):
    lane = lax.broadcasted_iota(jnp.int32, (rows, LANES), 1)
    low = lane < HEAD_DIM
    return [jnp.where(low, o[2 * p], o[2 * p + 1]) for p in range(GROUP // 2)]


def _swa_kernel(sink_ref, q_ref, kp_ref, kc_ref, vp_ref, vc_ref, o_ref):
    n = pl.program_id(1)
    blk = SWA_BLOCK
    qi = lax.broadcasted_iota(jnp.int32, (blk, 2 * blk), 0)
    si = lax.broadcasted_iota(jnp.int32, (blk, 2 * blk), 1)
    diff = qi + blk - si
    ok = (diff >= 0) & (diff < SWA_WINDOW) & ((si >= blk) | (n > 0))
    for g in range(KV_HEADS):
        q4 = _stack_heads(q_ref.at[:, g * 2 * LANES:(g + 1) * 2 * LANES], blk)
        kk = jnp.concatenate([kp_ref[:, g * LANES:(g + 1) * LANES], kc_ref[:, g * LANES:(g + 1) * LANES]], axis=0)
        vv = jnp.concatenate([vp_ref[:, g * LANES:(g + 1) * LANES], vc_ref[:, g * LANES:(g + 1) * LANES]], axis=0)
        s = _dot_nt(q4, kk).reshape(GROUP, blk, 2 * blk)
        s = jnp.where(ok[None], s, NEG_INF)
        ps, rs = [], []
        for h in range(GROUP):
            sink = sink_ref[g * GROUP + h] * LOG2E
            m = jnp.maximum(jnp.max(s[h], axis=-1, keepdims=True), sink)
            p = jnp.exp2(s[h] - m)
            rs.append(1.0 / (jnp.sum(p, axis=-1, keepdims=True) + jnp.exp2(sink - m)))
            ps.append(p.astype(BF16))
        o = _dot(jnp.concatenate(ps, axis=0), vv).reshape(GROUP, blk, LANES)
        o = [o[h] * rs[h] for h in range(GROUP)]
        for p, blkout in enumerate(_unstack_heads(o, blk)):
            c0 = (g * 2 + p) * LANES
            o_ref[:, c0:c0 + LANES] = blkout.astype(BF16)


def _swa_attention(q, k2, v2, sinks, batch, seq):
    t, nq = q.shape
    nk = k2.shape[1]
    nb = seq // SWA_BLOCK
    cur = lambda b, n: (b * nb + n, 0)
    prev = lambda b, n: (b * nb + jnp.maximum(n - 1, 0), 0)
    return pl.pallas_call(
        _swa_kernel,
        grid=(batch, nb),
        in_specs=[
            pl.BlockSpec(memory_space=pltpu.SMEM),
            pl.BlockSpec((SWA_BLOCK, nq), cur),
            pl.BlockSpec((SWA_BLOCK, nk), prev),
            pl.BlockSpec((SWA_BLOCK, nk), cur),
            pl.BlockSpec((SWA_BLOCK, nk), prev),
            pl.BlockSpec((SWA_BLOCK, nk), cur),
        ],
        out_specs=pl.BlockSpec((SWA_BLOCK, nq), cur),
        out_shape=jax.ShapeDtypeStruct((t, nq), BF16),
        compiler_params=_cparams("parallel", "parallel"),
        name="swa_attention",
    )(sinks, q, k2, k2, v2, v2)


def _outproj_ln_kernel(alpha, o_ref, w_ref, x_ref, g_ref, b_ref, y_ref, ybf_ref):
    z = alpha * x_ref[...].astype(F32) + _dot(o_ref[...], w_ref[...])
    y = _layer_norm(z, g_ref[...], b_ref[...])
    y_ref[...] = y
    ybf_ref[...] = y.astype(BF16)


def _outproj_ln_router_kernel(alpha, o_ref, w_ref, x_ref, g_ref, b_ref, r_ref, y_ref, ybf_ref, route_ref):
    z = alpha * x_ref[...].astype(F32) + _dot(o_ref[...], w_ref[...])
    y = _layer_norm(z, g_ref[...], b_ref[...])
    y_ref[...] = y
    ybf_ref[...] = y.astype(BF16)
    ne = r_ref.shape[0] // 2
    y_hi = y.astype(BF16)
    y_lo = (y - y_hi.astype(F32)).astype(BF16)
    part = _dot_nt(r_ref[...], y_hi)
    logits = part[0:ne] + part[ne:2 * ne] + _dot_nt(r_ref[0:ne, :], y_lo)
    eid = lax.broadcasted_iota(jnp.int32, logits.shape, 0)
    m1 = jnp.max(logits, axis=0, keepdims=True)
    i1 = jnp.min(jnp.where(logits == m1, eid, ne), axis=0, keepdims=True)
    rest = jnp.where(eid == i1, -jnp.inf, logits)
    m2 = jnp.max(rest, axis=0, keepdims=True)
    i2 = jnp.min(jnp.where(rest == m2, eid, ne), axis=0, keepdims=True)
    e = jnp.exp(m2 - m1)
    w1 = 1.0 / (1.0 + e)
    w2 = e / (1.0 + e)
    route_ref[...] = jnp.concatenate(
        [i1.astype(F32), i2.astype(F32), w1, w2, jnp.zeros((4, logits.shape[1]), F32)], axis=0)


def _outproj_ln(o, w, x, gain, bias, alpha, router_t=None, tm=512):
    t, d = x.shape
    row = lambda i: (i, 0)
    const = lambda i: (0, 0)
    in_specs = [
        pl.BlockSpec((tm, o.shape[1]), row),
        pl.BlockSpec(w.shape, const),
        pl.BlockSpec((tm, d), row),
        pl.BlockSpec((1, d), const),
        pl.BlockSpec((1, d), const),
    ]
    out_specs = [pl.BlockSpec((tm, d), row), pl.BlockSpec((tm, d), row)]
    out_shape = [jax.ShapeDtypeStruct((t, d), F32), jax.ShapeDtypeStruct((t, d), BF16)]
    args = [o, w, x, gain.reshape(1, d), bias.reshape(1, d)]
    if router_t is None:
        body = functools.partial(_outproj_ln_kernel, alpha)
        name = "outproj_ln"
    else:
        body = functools.partial(_outproj_ln_router_kernel, alpha)
        name = "outproj_ln_router"
        in_specs.append(pl.BlockSpec(router_t.shape, const))
        args.append(router_t)
        out_specs.append(pl.BlockSpec((8, tm), lambda i: (0, i)))
        out_shape.append(jax.ShapeDtypeStruct((8, t), F32))
    return pl.pallas_call(
        body,
        grid=(t // tm,),
        in_specs=in_specs,
        out_specs=out_specs,
        out_shape=out_shape,
        compiler_params=_cparams("parallel"),
        name=name,
    )(*args)


def _ffn_ln_kernel(alpha, xbf_ref, x_ref, wg_ref, wu_ref, wd_ref, g_ref, b_ref, y_ref, ybf_ref, acc_ref):
    j = pl.program_id(1)

    @pl.when(j == 0)
    def _():
        acc_ref[...] = jnp.zeros_like(acc_ref)

    xb = xbf_ref[...]
    h = jax.nn.silu(_dot(xb, wg_ref[...])) * _dot(xb, wu_ref[...])
    acc_ref[...] += _dot(h.astype(BF16), wd_ref[...])

    @pl.when(j == pl.num_programs(1) - 1)
    def _():
        y = _layer_norm(alpha * x_ref[...] + acc_ref[...], g_ref[...], b_ref[...])
        y_ref[...] = y
        ybf_ref[...] = y.astype(BF16)


def _ffn_ln(xbf, x, wg, wu, wd, gain, bias, alpha, tm=1024, tf=512):
    t, d = x.shape
    dff = wg.shape[1]
    row = lambda i, j: (i, 0)
    const = lambda i, j: (0, 0)
    return pl.pallas_call(
        functools.partial(_ffn_ln_kernel, alpha),
        grid=(t // tm, dff // tf),
        in_specs=[
            pl.BlockSpec((tm, d), row),
            pl.BlockSpec((tm, d), row),
            pl.BlockSpec((d, tf), lambda i, j: (0, j)),
            pl.BlockSpec((d, tf), lambda i, j: (0, j)),
            pl.BlockSpec((tf, d), lambda i, j: (j, 0)),
            pl.BlockSpec((1, d), const),
            pl.BlockSpec((1, d), const),
        ],
        out_specs=[pl.BlockSpec((tm, d), row), pl.BlockSpec((tm, d), row)],
        out_shape=[jax.ShapeDtypeStruct((t, d), F32), jax.ShapeDtypeStruct((t, d), BF16)],
        scratch_shapes=[pltpu.VMEM((tm, d), F32)],
        compiler_params=_cparams("parallel", "arbitrary"),
        name="ffn_ln",
    )(xbf, x, wg, wu, wd, gain.reshape(1, d), bias.reshape(1, d))


def _gelu_tanh(x):
    return 0.5 * x * (1.0 + jnp.tanh(math.sqrt(2.0 / math.pi) * (x + 0.044715 * (x * x * x))))


def _compress_kernel(kc_ref, vc_ref, pk_ref, pv_ref, w1k_ref, w1v_ref, w2k_ref, w2v_ref, ko_ref, vo_ref):
    nseg = ko_ref.shape[0]
    half = CMP_LEN // 2

    def one(t_ref, pos_ref, w1_ref, w2_ref, out_ref):
        hidden = w1_ref.shape[2]
        lo = jnp.zeros((nseg, hidden), F32)
        hi = jnp.zeros((nseg, hidden), F32)
        for l in range(half):
            rows = t_ref[pl.ds(l, nseg, stride=CMP_STRIDE), :]
            lo += _dot((rows + pos_ref[l:l + 1, :]).astype(BF16), w1_ref[l])
            hi += _dot((rows + pos_ref[half + l:half + l + 1, :]).astype(BF16), w1_ref[half + l])
        h = _gelu_tanh(lo + pltpu.roll(hi, nseg - 1, 0))
        out_ref[...] = _dot(h.astype(BF16), w2_ref[...]).astype(out_ref.dtype)

    one(kc_ref, pk_ref, w1k_ref, w2k_ref, ko_ref)
    one(vc_ref, pv_ref, w1v_ref, w2v_ref, vo_ref)


def _compress(kc, vc, pk, pv, w1k, w1v, w2k, w2v, batch, seq):
    nseg = seq // CMP_STRIDE
    npair = KV_HEADS // 2
    tok = lambda b, p: (b, p)
    c2 = lambda b, p: (0, 0)
    c3 = lambda b, p: (0, 0, 0)
    return pl.pallas_call(
        _compress_kernel,
        grid=(batch, npair),
        in_specs=[
            pl.BlockSpec((seq, LANES), tok),
            pl.BlockSpec((seq, LANES), tok),
            pl.BlockSpec(pk.shape, c2),
            pl.BlockSpec(pv.shape, c2),
            pl.BlockSpec(w1k.shape, c3),
            pl.BlockSpec(w1v.shape, c3),
            pl.BlockSpec(w2k.shape, c2),
            pl.BlockSpec(w2v.shape, c2),
        ],
        out_specs=[pl.BlockSpec((nseg, 2 * LANES), tok), pl.BlockSpec((nseg, 2 * LANES), tok)],
        out_shape=[jax.ShapeDtypeStruct((batch * nseg, KV_HEADS * LANES), BF16)] * 2,
        compiler_params=_cparams("parallel", "parallel"),
        name="nsa_compress",
    )(kc, vc, pk, pv, w1k, w1v, w2k, w2v)


def _lane_fold(x, op):
    out = x[:, 0:LANES]
    for j in range(1, x.shape[1] // LANES):
        out = op(out, x[:, j * LANES:(j + 1) * LANES])
    return out


def _nsa_kernel(q_ref, gate_ref, kcc_ref, vcc_ref, ks_ref, vs_ref, kw_ref, vw_ref, ovt_ref, blk_ref,
                o_ref, qa_ref, imp_ref, s_ref, mx_ref, l_ref, acc_ref, part_ref):
    tq, kc = NSA_TQ, NSA_KC
    rows = GROUP * tq
    ncp = kcc_ref.shape[0]
    nsel = ovt_ref.shape[0]
    topn = min(SEL_TOPN, nsel)
    t0 = pl.program_id(2) * tq
    qa_ref[:, 0:LANES] = _stack_heads(q_ref, tq)
    tpos = t0 + lax.broadcasted_iota(jnp.int32, (tq, 1), 0)
    gates = gate_ref[...]

    def gate_column(branch):
        return jnp.concatenate([gates[:, N_GATES * h + branch:N_GATES * h + branch + 1] for h in range(GROUP)],
                               axis=0)

    cend = CMP_STRIDE * lax.broadcasted_iota(jnp.int32, (1, ncp), 1) + (CMP_LEN - 1)
    ok_c = (cend <= tpos)[None]
    s = jnp.where(ok_c, _dot_nt(qa_ref[:, 0:LANES], kcc_ref[...]).reshape(GROUP, tq, ncp), NEG_INF)
    p = jnp.where(ok_c, jnp.exp2(s - jnp.max(s, axis=-1, keepdims=True)), 0.0)
    p = p * (1.0 / jnp.maximum(jnp.sum(p, axis=-1, keepdims=True), 1e-30))
    o_c = _dot(p.reshape(rows, ncp).astype(BF16), vcc_ref[...])
    psum = p[0] + p[1] + p[2] + p[3]

    span = NSA_WINDOW + tq
    w0 = pl.multiple_of(jnp.maximum(t0 - NSA_WINDOW, 0), LANES)
    d = tpos - (w0 + lax.broadcasted_iota(jnp.int32, (1, span), 1))
    bias = jnp.where((d >= 0) & (d < NSA_WINDOW), 0.0, NEG_INF)
    s = _dot_nt(qa_ref[:, 0:LANES], kw_ref[pl.ds(w0, span), :])
    s = (s.reshape(GROUP, tq, span) + bias[None]).reshape(rows, span)
    mw = jnp.broadcast_to(jnp.max(_lane_fold(s, jnp.maximum), axis=-1, keepdims=True), (rows, LANES))
    p = [jnp.exp2(s[:, j * LANES:(j + 1) * LANES] - mw) for j in range(span // LANES)]
    lw = jnp.sum(functools.reduce(jnp.add, p), axis=-1, keepdims=True)
    o_w = _dot(jnp.concatenate([x.astype(BF16) for x in p], axis=1), vw_ref[pl.ds(w0, span), :])
    part_ref[...] = gate_column(0) * o_c + (gate_column(2) / lw) * o_w

    ovt = ovt_ref[...]
    imp = jnp.zeros((nsel, tq), F32)
    rem = psum
    for _ in range(3):
        part = rem.astype(BF16)
        imp += _dot_nt(ovt, part)
        rem = rem - part.astype(F32)
    blk = lax.broadcasted_iota(jnp.int32, (nsel, tq), 0)
    cur = jnp.right_shift(t0 + lax.broadcasted_iota(jnp.int32, (nsel, tq), 1), SEL_SHIFT)
    forced = (blk == 0) | (blk == cur) | (blk == cur - 1)
    imp = jnp.where(forced, SEL_FORCE, imp)
    imp = jnp.where(blk <= cur, imp, -1.0)
    imp_ref[...] = imp

    ngrp = nsel // 8
    sub = lax.broadcasted_iota(jnp.int32, (8, tq), 0)
    grp = [imp[8 * r:8 * r + 8] for r in range(ngrp)]
    cnt = [jnp.zeros((8, tq), F32) for _ in range(ngrp)]
    for i in range(nsel):
        row = jnp.broadcast_to(imp_ref[i:i + 1, :], (8, tq))
        for r in range(ngrp):
            ge = jnp.where(row >= grp[r], 1.0, 0.0)
            gt = jnp.where(row > grp[r], 1.0, 0.0)
            if 8 * r > i:
                cnt[r] = cnt[r] + ge
            elif 8 * r + 7 < i:
                cnt[r] = cnt[r] + gt
            else:
                cnt[r] = cnt[r] + jnp.where(sub > i - 8 * r, ge, gt)
    rank = jnp.concatenate(cnt, axis=0)
    unsel = jnp.where((rank < topn) & (blk <= cur), 0.0, -1.0)
    if nsel < LANES:
        unsel = jnp.concatenate([unsel, jnp.zeros((LANES - nsel, tq), F32)], axis=0)
    selq = unsel.T.astype(BF16)
    for h in range(GROUP):
        qa_ref[h * tq:(h + 1) * tq, LANES:2 * LANES] = selq

    def key_side(start):
        return jnp.concatenate([ks_ref[pl.ds(start, kc), :], blk_ref[pl.ds(start, kc), :]], axis=1)

    nfull = t0 // kc
    dstart = pl.multiple_of(nfull * kc, kc)
    causal = jnp.where(dstart + lax.broadcasted_iota(jnp.int32, (1, kc), 1) <= tpos, 0.0, NEG_INF)
    s = (_dot_nt(qa_ref[...], key_side(dstart)).reshape(GROUP, tq, kc) + causal[None]).reshape(rows, kc)
    s_ref[nfull] = s
    mx_ref[...] = _lane_fold(s, jnp.maximum)

    def score_chunks(first, count):
        mx = mx_ref[...]
        for u in range(count):
            s = _dot_nt(qa_ref[...], key_side(pl.multiple_of((first + u) * kc, kc)))
            s_ref[first + u] = s
            mx = jnp.maximum(mx, _lane_fold(s, jnp.maximum))
        mx_ref[...] = mx

    def score_pair(c, carry):
        score_chunks(2 * c, 2)
        return carry

    lax.fori_loop(0, nfull // 2, score_pair, 0)

    @pl.when(nfull % 2 == 1)
    def _():
        score_chunks(nfull - 1, 1)

    mx_ref[...] = jnp.broadcast_to(jnp.max(mx_ref[...], axis=-1, keepdims=True), (rows, LANES))
    l_ref[...] = jnp.zeros(l_ref.shape, F32)
    acc_ref[...] = jnp.zeros(acc_ref.shape, F32)

    def value_chunks(first, count):
        mrep = mx_ref[...]
        lsum = l_ref[...]
        acc = acc_ref[...]
        for u in range(count):
            sc = s_ref[first + u]
            p = [jnp.exp2(sc[:, j * LANES:(j + 1) * LANES] - mrep) for j in range(kc // LANES)]
            lsum = lsum + functools.reduce(jnp.add, p)
            pb = jnp.concatenate([x.astype(BF16) for x in p], axis=1)
            acc = acc + _dot(pb, vs_ref[pl.ds(pl.multiple_of((first + u) * kc, kc), kc), :])
        l_ref[...] = lsum
        acc_ref[...] = acc

    def value_pair(c, carry):
        value_chunks(2 * c, 2)
        return carry

    lax.fori_loop(0, (nfull + 1) // 2, value_pair, 0)

    @pl.when(nfull % 2 == 0)
    def _():
        value_chunks(nfull, 1)

    ls = jnp.sum(l_ref[...], axis=-1, keepdims=True)
    out = part_ref[...] + (gate_column(1) / ls) * acc_ref[...]
    for pidx, blkout in enumerate(_unstack_heads(out.reshape(GROUP, tq, LANES), tq)):
        o_ref[:, pidx * LANES:(pidx + 1) * LANES] = blkout.astype(BF16)


def _nsa_attention(q, gates, kcc, vcc, ks2, vs2, kw2, vw2, ovt, key_blk, batch, seq):
    t, nq = q.shape
    tq = NSA_TQ
    rows = GROUP * tq
    nt = seq // tq
    ncp = seq // CMP_STRIDE
    nsel = seq // SEL_BLOCK
    qmap = lambda b, g, i: (b * nt + i, g)
    kvmap = lambda b, g, i: (b, g)
    return pl.pallas_call(
        _nsa_kernel,
        grid=(batch, KV_HEADS, nt),
        in_specs=[
            pl.BlockSpec((tq, 2 * LANES), qmap),
            pl.BlockSpec((tq, LANES), qmap),
            pl.BlockSpec((ncp, LANES), kvmap),
            pl.BlockSpec((ncp, LANES), kvmap),
            pl.BlockSpec((seq, LANES), kvmap),
            pl.BlockSpec((seq, LANES), kvmap),
            pl.BlockSpec((seq, LANES), kvmap),
            pl.BlockSpec((seq, LANES), kvmap),
            pl.BlockSpec(ovt.shape, lambda b, g, i: (0, 0)),
            pl.BlockSpec(key_blk.shape, lambda b, g, i: (0, 0)),
        ],
        out_specs=pl.BlockSpec((tq, 2 * LANES), qmap),
        out_shape=jax.ShapeDtypeStruct((t, nq), BF16),
        scratch_shapes=[
            pltpu.VMEM((rows, 2 * LANES), BF16),
            pltpu.VMEM((nsel, tq), F32),
            pltpu.VMEM((seq // NSA_KC, rows, NSA_KC), F32),
            pltpu.VMEM((rows, LANES), F32),
            pltpu.VMEM((rows, LANES), F32),
            pltpu.VMEM((rows, LANES), F32),
            pltpu.VMEM((rows, LANES), F32),
        ],
        compiler_params=_cparams("parallel", "parallel", "arbitrary"),
        name="nsa_attention",
    )(q, gates, kcc, vcc, ks2, vs2, kw2, vw2, ovt, key_blk)


def _moe_kernel(te_ref, nu_ref, xs_ref, wg_ref, wu_ref, wd_ref, ys_ref, acc_ref):
    i = pl.program_id(0)
    j = pl.program_id(1)

    @pl.when(i < nu_ref[0])
    def _():
        @pl.when(j == 0)
        def _():
            acc_ref[...] = jnp.zeros_like(acc_ref)

        xb = xs_ref[...]
        h = jax.nn.silu(_dot(xb, wg_ref[0].astype(BF16))) * _dot(xb, wu_ref[0].astype(BF16))
        acc_ref[...] += _dot(h.astype(BF16), wd_ref[0].astype(BF16))

        @pl.when(j == pl.num_programs(1) - 1)
        def _():
            ys_ref[...] = acc_ref[...].astype(ys_ref.dtype)

    @pl.when((i >= nu_ref[0]) & (j == pl.num_programs(1) - 1))
    def _():
        ys_ref[...] = jnp.zeros_like(ys_ref)


def _moe_ffn(xs, wg, wu, wd, tile_expert, n_used, tm, tf=512):
    p, d = xs.shape
    dff = wg.shape[2]
    nj = dff // tf

    def rows(i, j, te, nu):
        return (jnp.minimum(i, nu[0] - 1), 0)

    def jj(i, j, nu):
        return jnp.where(i < nu[0], j, nj - 1)

    return pl.pallas_call(
        _moe_kernel,
        grid_spec=pltpu.PrefetchScalarGridSpec(
            num_scalar_prefetch=2,
            grid=(p // tm, nj),
            in_specs=[
                pl.BlockSpec((tm, d), rows),
                pl.BlockSpec((1, d, tf), lambda i, j, te, nu: (te[i], 0, jj(i, j, nu))),
                pl.BlockSpec((1, d, tf), lambda i, j, te, nu: (te[i], 0, jj(i, j, nu))),
                pl.BlockSpec((1, tf, d), lambda i, j, te, nu: (te[i], jj(i, j, nu), 0)),
            ],
            out_specs=pl.BlockSpec((tm, d), lambda i, j, te, nu: (i, 0)),
            scratch_shapes=[pltpu.VMEM((tm, d), F32)],
        ),
        out_shape=jax.ShapeDtypeStruct((p, d), BF16),
        compiler_params=_cparams("arbitrary", "arbitrary"),
        name="moe_ffn",
    )(tile_expert, n_used, xs, wg, wu, wd)


def _combine_ln_kernel(alpha, x_ref, y1_ref, y2_ref, w_ref, g_ref, b_ref, o_ref, obf_ref):
    w = w_ref[...]
    f = w[:, 0:1] * y1_ref[...].astype(F32) + w[:, 1:2] * y2_ref[...].astype(F32)
    y = _layer_norm(alpha * x_ref[...] + f, g_ref[...], b_ref[...])
    o_ref[...] = y
    obf_ref[...] = y.astype(BF16)


def _combine_ln(x, y1, y2, wcol, gain, bias, alpha, tm=512):
    t, d = x.shape
    row = lambda i: (i, 0)
    const = lambda i: (0, 0)
    return pl.pallas_call(
        functools.partial(_combine_ln_kernel, alpha),
        grid=(t // tm,),
        in_specs=[
            pl.BlockSpec((tm, d), row),
            pl.BlockSpec((tm, d), row),
            pl.BlockSpec((tm, d), row),
            pl.BlockSpec((tm, 2), row),
            pl.BlockSpec((1, d), const),
            pl.BlockSpec((1, d), const),
        ],
        out_specs=[pl.BlockSpec((tm, d), row), pl.BlockSpec((tm, d), row)],
        out_shape=[jax.ShapeDtypeStruct((t, d), F32), jax.ShapeDtypeStruct((t, d), BF16)],
        compiler_params=_cparams("parallel"),
        name="combine_ln",
    )(x, y1, y2, wcol, gain.reshape(1, d), bias.reshape(1, d))


def _moe(x, xbf, route, wg, wu, wd, gain, bias, alpha, tm=1024):
    t, d = x.shape
    ne = wg.shape[0]
    ids = route[0:2].astype(jnp.int32)
    experts = jnp.arange(ne, dtype=jnp.int32)[None, :]
    hit1 = ids[0][:, None] == experts
    hit2 = ids[1][:, None] == experts
    csum = jnp.cumsum((hit1 | hit2).astype(jnp.int32), axis=0)
    counts = csum[-1]
    padded = ((counts + tm - 1) // tm) * tm
    ends = jnp.cumsum(padded)
    starts = ends - padded
    slot = starts[None, :] + csum - 1
    pos = jnp.concatenate([jnp.sum(jnp.where(hit1, slot, 0), axis=1), jnp.sum(jnp.where(hit2, slot, 0), axis=1)])
    n_rows = 2 * t + ne * tm
    n_tiles = n_rows // tm
    tile_start = jnp.arange(n_tiles, dtype=jnp.int32) * tm
    tile_expert = jnp.minimum(jnp.sum((tile_start[:, None] >= ends[None, :]).astype(jnp.int32), axis=1), ne - 1)
    n_used = (ends[-1] // tm).astype(jnp.int32)
    tok = jnp.tile(jnp.arange(t, dtype=jnp.int32), 2)
    _, tok_sorted = lax.sort_key_val(pos, tok)
    tok_sorted = jnp.concatenate([tok_sorted, jnp.arange(tm, dtype=jnp.int32)])
    first = (jnp.cumsum(counts) - counts)[tile_expert] + tile_start - starts[tile_expert]
    first = jnp.clip(first, 0, 2 * t)
    rows_of_tile = first[:, None] + jnp.arange(tm, dtype=jnp.int32)[None, :]
    src = jnp.take(tok_sorted, rows_of_tile.reshape(n_rows), mode="clip")
    last_expert = tile_expert[jnp.maximum(n_used - 1, 0)]
    tile_expert = jnp.where(jnp.arange(n_tiles) < n_used, tile_expert, last_expert).astype(jnp.int32)
    xs = jnp.take(xbf, src, axis=0, mode="clip")
    ys = _moe_ffn(xs, wg, wu, wd, tile_expert, n_used.reshape(1), tm)
    y1 = jnp.take(ys, pos[:t], axis=0, mode="clip")
    y2 = jnp.take(ys, pos[t:], axis=0, mode="clip")
    wcol = route[2:4].T
    return _combine_ln(x, y1, y2, wcol, gain, bias, alpha)


def _prep_a_w_in(w):
    d = w.shape[0]
    nq = d
    nkv = KV_HEADS * HEAD_DIM
    q = w[:, :nq][:, _pair_cols(nq // HEAD_DIM)]
    k = w[:, nq:nq + nkv][:, _pair_cols(KV_HEADS)]
    v = w[:, nq + nkv:]
    return jnp.concatenate([q, k, v], axis=1).astype(BF16)


def _prep_b_w_in(w):
    d = w.shape[0]
    nq = d
    nkv = KV_HEADS * HEAD_DIM
    part = lambda i: w[:, nq + i * nkv:nq + (i + 1) * nkv]
    q = w[:, :nq][:, _pair_cols(nq // HEAD_DIM)]
    kc = part(0)[:, _pair_cols(KV_HEADS)]
    vc = part(1)
    ks = part(2)[:, _pair_cols(KV_HEADS)]
    vs = part(3)
    kw = part(4)[:, _pair_cols(KV_HEADS)]
    vw = part(5)
    gl = w[:, nq + 6 * nkv:]
    per = GROUP * N_GATES
    gcols = []
    for g in range(KV_HEADS):
        gcols.append(jnp.pad(gl[:, g * per:(g + 1) * per], ((0, 0), (0, LANES - per))))
    return jnp.concatenate([q, kc, ks, kw, vc, vs, vw] + gcols, axis=1).astype(BF16)


def _prep_compress(pos, w1, w2, rope_layout):
    hidden = w1.shape[1]
    w1 = w1.reshape(CMP_LEN, HEAD_DIM, hidden)
    zeros = jnp.zeros_like(w1)
    a = jnp.concatenate([w1, zeros], axis=2)
    b = jnp.concatenate([zeros, w1], axis=2)
    natural = jnp.concatenate([a, b], axis=1)
    pos2 = jnp.concatenate([pos, pos], axis=1)
    if rope_layout:
        cols = _pair_cols(2)
        natural, pos2 = natural[:, cols, :], pos2[:, cols]
        out_cols = _kdup_cols(1)
    else:
        out_cols = _vdup_cols(1)
    w2d = w2[:, out_cols]
    z2 = jnp.zeros_like(w2d)
    w2p = jnp.concatenate([jnp.concatenate([w2d, z2], axis=1), jnp.concatenate([z2, w2d], axis=1)], axis=0)
    return pos2.astype(F32), natural.astype(BF16), w2p.astype(BF16)


def _split_router(router):
    rt = router.T
    hi = rt.astype(BF16)
    lo = (rt - hi.astype(F32)).astype(BF16)
    return jnp.concatenate([hi, lo], axis=0)


def _overlap_t(ncp, nsel):
    cs = CMP_STRIDE * np.arange(ncp)
    ce = cs + CMP_LEN
    ss = SEL_BLOCK * np.arange(nsel)
    se = ss + SEL_BLOCK
    ov = np.clip(np.minimum(ce[None, :], se[:, None]) - np.maximum(cs[None, :], ss[:, None]), 0, None)
    return jnp.asarray(ov / CMP_STRIDE, dtype=BF16)


def _key_block_indicator(seq):
    assert seq // SEL_BLOCK <= LANES
    hit = (np.arange(seq) // SEL_BLOCK)[:, None] == np.arange(LANES)[None, :]
    return jnp.asarray(np.where(hit, SEL_MASK, 0.0), dtype=BF16)


def kernel(x, a_w_in, a_w_out, a_sinks, b_w_in, b_w_out, b_cmp_pos_k, b_cmp_pos_v, b_cmp_k_w1, b_cmp_k_w2, b_cmp_v_w1, b_cmp_v_w2, ffn_w_gate, ffn_w_up, ffn_w_down, moe_router, moe_w_gate, moe_w_up, moe_w_down, ln_gain, ln_bias):
    batch, seq, d = x.shape
    depth = ln_gain.shape[0]
    alpha = float((2 * depth) ** 0.25)
    tables = _rope_tables(seq)
    xf = x.reshape(batch * seq, d)
    xbf = xf
    for i in range(depth):
        j = i // 2
        if i % 2 == 0:
            q, k2, v2 = _proj_a(xbf, _prep_a_w_in(a_w_in[j]), tables, seq)
            o = _swa_attention(q, k2, v2, a_sinks[j], batch, seq)
            xf, xbf = _outproj_ln(o, a_w_out[j].astype(BF16), xf, ln_gain[i, 0], ln_bias[i, 0], alpha)
            xf, xbf = _ffn_ln(xbf, xf, ffn_w_gate[j].astype(BF16), ffn_w_up[j].astype(BF16),
                              ffn_w_down[j].astype(BF16), ln_gain[i, 1], ln_bias[i, 1], alpha)
        else:
            q, kc, ks2, kw2, vc, vs2, vw2, gates = _proj_b(xbf, _prep_b_w_in(b_w_in[j]), tables, seq)
            pk, w1k, w2k = _prep_compress(b_cmp_pos_k[j], b_cmp_k_w1[j], b_cmp_k_w2[j], True)
            pv, w1v, w2v = _prep_compress(b_cmp_pos_v[j], b_cmp_v_w1[j], b_cmp_v_w2[j], False)
            kcc, vcc = _compress(kc, vc, pk, pv, w1k, w1v, w2k, w2v, batch, seq)
            ovt = _overlap_t(seq // CMP_STRIDE, seq // SEL_BLOCK)
            o = _nsa_attention(q, gates, kcc, vcc, ks2, vs2, kw2, vw2, ovt,
                               _key_block_indicator(seq), batch, seq)
            xf, xbf, route = _outproj_ln(o, b_w_out[j].astype(BF16), xf, ln_gain[i, 0], ln_bias[i, 0], alpha,
                                         router_t=_split_router(moe_router[j]))
            xf, xbf = _moe(xf, xbf, route, moe_w_gate[j], moe_w_up[j], moe_w_down[j],
                           ln_gain[i, 1], ln_bias[i, 1], alpha)
    return xf.reshape(batch, seq, d)
```
